```python
import math
import jax, jax.numpy as jnp
from jax import lax
import numpy as np

D_MODEL = 1024
BATCH = 16
SEQ = 2048
DEPTH = 1
DEC_BATCH = 32
DEC_SEQ = 16
PAST_LEN = 4096

CHUNK = 64
Q_BLOCK = 128
MIX_WIDTH = D_MODEL
POOL_WIDTH = MIX_WIDTH // 2
POOL_WINDOWS = (2, 4, 8, 16)
N_POOL_GROUPS = len(POOL_WINDOWS)
POOL_GROUP = POOL_WIDTH // N_POOL_GROUPS
POOL_STATE = max(POOL_WINDOWS) - 1
ATTN_WIDTH = MIX_WIDTH - POOL_WIDTH
N_HEADS = 4
HEAD_DIM = ATTN_WIDTH // (2 * N_HEADS)
V_DIM = 2 * HEAD_DIM
QK_WIDTH = N_HEADS * 2 * HEAD_DIM
IN_WIDTH = POOL_WIDTH + 2 * QK_WIDTH + N_HEADS * V_DIM
D_FF = 4 * D_MODEL
EPS = 1e-6
SUBLN_EPS = 1e-5

kernel_name = "hybrid_pool_diffattn_stream_step"


def rms_norm(x, g, eps=EPS):
    xf = x.astype(jnp.float32)
    y = xf * lax.rsqrt(jnp.mean(xf * xf, axis=-1, keepdims=True) + eps)
    return (y * g.astype(jnp.float32)).astype(x.dtype)


def alibi_slopes():
    return 2.0 ** (-8.0 * jnp.arange(1, N_HEADS + 1, dtype=jnp.float32) / N_HEADS)


def pool_mix(u, buf, pos0, w_pool, pool_scale):
    B, T, C = u.shape
    full = jnp.concatenate([buf.astype(u.dtype), u], axis=1)
    cs = jnp.cumsum(full.astype(jnp.float32), axis=1)
    cs = jnp.concatenate([jnp.zeros((B, 1, C), jnp.float32), cs], axis=1)
    end = cs[:, POOL_STATE + 1:]
    pos = pos0 + jnp.arange(T, dtype=jnp.int32)
    means = []
    for g, w in enumerate(POOL_WINDOWS):
        sl = slice(g * POOL_GROUP, (g + 1) * POOL_GROUP)
        start = cs[:, POOL_STATE + 1 - w: POOL_STATE + 1 - w + T, sl]
        cnt = jnp.minimum(pos + 1, w).astype(jnp.float32)[None, :, None]
        means.append((end[..., sl] - start) / cnt)
    mean = jnp.stack(means, axis=2)
    diff = mean - u.reshape(B, T, N_POOL_GROUPS, POOL_GROUP).astype(jnp.float32)
    y = jnp.einsum('btgc,gcd->btgd', diff, w_pool.astype(jnp.float32)).reshape(B, T, C)
    y = y * pool_scale.astype(jnp.float32)
    return y.astype(u.dtype), full[:, -POOL_STATE:]


def diff_attend(q, k, v, q_pos, k_pos, lam):
    s = jnp.einsum('bqhcd,bkhcd->bhcqk', q.astype(jnp.float32), k.astype(jnp.float32))
    s = s * (1.0 / math.sqrt(HEAD_DIM))
    dist = jnp.abs(q_pos[:, None] - k_pos[None, :]).astype(jnp.float32)
    bias = -alibi_slopes()[:, None, None, None] * dist[None, None]
    visible = (k_pos[None, :] // CHUNK) <= (q_pos[:, None] // CHUNK)
    s = jnp.where(visible, s + bias, -jnp.inf)
    p = jax.nn.softmax(s, axis=-1)
    a = p[:, :, 0] - lam * p[:, :, 1]
    return jnp.einsum('bhqk,bkhv->bqhv', a, v.astype(jnp.float32))


def token_mixer(h, pool_buf, past_k, past_v, pos0, w_in, w_pool, pool_scale, lam, lambda_init, subln_g, w_out):
    B, T, _ = h.shape
    proj = h @ w_in
    u = proj[..., :POOL_WIDTH]
    q = proj[..., POOL_WIDTH:POOL_WIDTH + QK_WIDTH].reshape(B, T, N_HEADS, 2, HEAD_DIM)
    k = proj[..., POOL_WIDTH + QK_WIDTH:POOL_WIDTH + 2 * QK_WIDTH].reshape(B, T, N_HEADS, 2, HEAD_DIM)
    v = proj[..., POOL_WIDTH + 2 * QK_WIDTH:].reshape(B, T, N_HEADS, V_DIM)
    pool_out, new_buf = pool_mix(u, pool_buf, pos0, w_pool, pool_scale)
    pos_new = pos0 + jnp.arange(T, dtype=jnp.int32)
    if past_k is None:
        outs = []
        for blk in range(T // Q_BLOCK):
            q0, q1 = blk * Q_BLOCK, (blk + 1) * Q_BLOCK
            outs.append(diff_attend(q[:, q0:q1], k[:, :q1], v[:, :q1], pos_new[q0:q1], pos_new[:q1], lam))
        o = jnp.concatenate(outs, axis=1)
    else:
        k_all = jnp.concatenate([past_k.astype(k.dtype), k], axis=1)
        v_all = jnp.concatenate([past_v.astype(v.dtype), v], axis=1)
        k_pos = jnp.arange(past_k.shape[1] + T, dtype=jnp.int32)
        o = diff_attend(q, k_all, v_all, pos_new, k_pos, lam)
    o = rms_norm(o, subln_g, SUBLN_EPS) * (1.0 - lambda_init)
    mixed = jnp.concatenate([pool_out, o.reshape(B, T, ATTN_WIDTH).astype(h.dtype)], axis=-1)
    return mixed @ w_out, new_buf, k, v


def sq_relu_mlp(h, w_up, w_down):
    a = jax.nn.relu(h @ w_up)
    return (a * a) @ w_down


def layer(x, pool_buf, past_k, past_v, pos0, norm_mix_g, w_in, w_pool, pool_scale, lam, lambda_init,
          subln_g, w_out, norm_ffn_g, w_up, w_down):
    m, new_buf, k_new, v_new = token_mixer(rms_norm(x, norm_mix_g), pool_buf, past_k, past_v, pos0,
                                           w_in, w_pool, pool_scale, lam, lambda_init, subln_g, w_out)
    x = x + m
    x = x + sq_relu_mlp(rms_norm(x, norm_ffn_g), w_up, w_down)
    return x, new_buf, k_new, v_new


def setup_inputs(seed: int = 0) -> dict:
    key = jax.random.key(seed)
    ks = jax.random.split(key, 20)
    f32 = jnp.float32
    nrm = lambda k, s, sc=1.0: (jax.random.normal(k, s, f32) * sc)
    return {
        "x_prompt": nrm(ks[0], (BATCH, SEQ, D_MODEL)),
        "x_sample": nrm(ks[1], (DEC_BATCH, DEC_SEQ, D_MODEL)),
        "state_pool": nrm(ks[2], (DEPTH, DEC_BATCH, POOL_STATE, POOL_WIDTH)),
        "cache_k": nrm(ks[3], (DEPTH, DEC_BATCH, PAST_LEN, N_HEADS, 2, HEAD_DIM)),
        "cache_v": nrm(ks[4], (DEPTH, DEC_BATCH, PAST_LEN, N_HEADS, V_DIM)),
        "norm_mix_g": 1.0 + nrm(ks[5], (DEPTH, D_MODEL), 0.02),
        "w_in": nrm(ks[6], (DEPTH, D_MODEL, IN_WIDTH), D_MODEL ** -0.5),
        "w_pool": nrm(ks[7], (DEPTH, N_POOL_GROUPS, POOL_GROUP, POOL_GROUP), POOL_GROUP ** -0.5),
        "pool_scale": 1.0 + nrm(ks[8], (DEPTH, POOL_WIDTH), 0.1),
        "lambda_q1": nrm(ks[9], (DEPTH, HEAD_DIM), 0.1),
        "lambda_k1": nrm(ks[10], (DEPTH, HEAD_DIM), 0.1),
        "lambda_q2": nrm(ks[11], (DEPTH, HEAD_DIM), 0.1),
        "lambda_k2": nrm(ks[12], (DEPTH, HEAD_DIM), 0.1),
        "subln_g": 1.0 + nrm(ks[13], (DEPTH, V_DIM), 0.02),
        "w_out": nrm(ks[14], (DEPTH, MIX_WIDTH, D_MODEL), MIX_WIDTH ** -0.5),
        "norm_ffn_g": 1.0 + nrm(ks[15], (DEPTH, D_MODEL), 0.02),
        "w_up": nrm(ks[16], (DEPTH, D_MODEL, D_FF), D_MODEL ** -0.5),
        "w_down": nrm(ks[17], (DEPTH, D_FF, D_MODEL), D_FF ** -0.5),
        "norm_final_g": 1.0 + nrm(ks[18], (D_MODEL,), 0.02),
    }


def reference(x_prompt, x_sample, state_pool, cache_k, cache_v, norm_mix_g, w_in, w_pool, pool_scale,
              lambda_q1, lambda_k1, lambda_q2, lambda_k2, subln_g, w_out, norm_ffn_g, w_up, w_down,
              norm_final_g):
    yp, ys = x_prompt, x_sample
    pool_p, k_p, v_p, pool_s, k_s, v_s = [], [], [], [], [], []
    for l in range(DEPTH):
        lambda_init = 0.8 - 0.6 * math.exp(-0.3 * l)
        lam = (jnp.exp(jnp.sum(lambda_q1[l].astype(jnp.float32) * lambda_k1[l].astype(jnp.float32)))
               - jnp.exp(jnp.sum(lambda_q2[l].astype(jnp.float32) * lambda_k2[l].astype(jnp.float32)))
               + lambda_init)
        params = (norm_mix_g[l], w_in[l], w_pool[l], pool_scale[l], lam, lambda_init,
                  subln_g[l], w_out[l], norm_ffn_g[l], w_up[l], w_down[l])
        zero_buf = jnp.zeros((yp.shape[0], POOL_STATE, POOL_WIDTH), yp.dtype)
        yp, bp, kp, vp = layer(yp, zero_buf, None, None, 0, *params)
        ys, bs, kss, vs = layer(ys, state_pool[l], cache_k[l], cache_v[l], PAST_LEN, *params)
        pool_p.append(bp); k_p.append(kp); v_p.append(vp)
        pool_s.append(bs); k_s.append(kss); v_s.append(vs)
    y_prompt = rms_norm(yp, norm_final_g)
    y_sample = rms_norm(ys, norm_final_g)
    pool_prompt = jnp.stack(pool_p)
    k_prompt = jnp.stack(k_p)
    v_prompt = jnp.stack(v_p)
    pool_sample = jnp.stack(pool_s)
    k_sample = jnp.stack(k_s)
    v_sample = jnp.stack(v_s)
    return (y_prompt, y_sample, pool_prompt, k_prompt, v_prompt, pool_sample, k_sample, v_sample)
```

```python
import functools
import math

import jax
import jax.numpy as jnp
from jax import lax
from jax.experimental import pallas as pl
from jax.experimental.pallas import tpu as pltpu

D_MODEL = 1024
POOL_WIDTH = 512
POOL_WINDOWS = (2, 4, 8, 16)
POOL_GROUP = 128
POOL_STATE = 15
N_HEADS = 4
HEAD_DIM = 64
V_DIM = 128
QK_WIDTH = 512
V_WIDTH = N_HEADS * V_DIM
D_FF = 4096
CHUNK_BITS = 6
HEAD_DIM_BITS = 6
assert HEAD_DIM == 1 << HEAD_DIM_BITS
EPS = 1e-6
SUBLN_EPS = 1e-5

TAIL = 16
HEAD_LANES = 2 * HEAD_DIM

BF16 = jnp.bfloat16
F32 = jnp.float32

PROMPT_ROWS = 512
ATTN_BLOCK = 256
CACHE_BLOCK = 2048
FF_CHUNK = 1024
VMEM_LIMIT = 56 * 1024 * 1024


def _nt_dot(a, b):
    return lax.dot_general(a, b, (((1,), (1,)), ((), ())), preferred_element_type=F32)


def _dot(a, b):
    return jnp.dot(a, b, preferred_element_type=F32)


def _head_slope(h):
    return lax.bitcast_convert_type((127 - 2 * (h + 1)) << 23, F32)


def _lambda(lq1, lk1, lq2, lk2, lambda_init):
    return (jnp.exp(jnp.sum(lq1[...] * lk1[...], axis=-1, keepdims=True))
            - jnp.exp(jnp.sum(lq2[...] * lk2[...], axis=-1, keepdims=True)) + lambda_init)


def _sub_ln(o0, o1, lam, g, lambda_init):
    o = o0 - lam * o1
    ms = jnp.mean(o * o, axis=-1, keepdims=True)
    return (o * lax.rsqrt(ms + SUBLN_EPS) * g) * (1.0 - lambda_init)


def _proj_pool_kernel(x_ref, g_ref, w_in_ref, w_pool_ref, pscale_ref, buf_ref,
                      k_ref, v_ref, qb_ref, kb_ref, vb_ref, pool_ref, tail_ref, ext_ref,
                      *, pos0, feature_major_keys):
    bb, tm, _ = x_ref.shape
    rows = bb * tm
    t = pl.program_id(1)

    x = x_ref[...].reshape(rows, D_MODEL)
    ms = jnp.mean(x * x, axis=-1, keepdims=True)
    hn = (x * lax.rsqrt(ms + EPS) * g_ref[...]).astype(BF16)

    def proj(j):
        return _dot(hn, w_in_ref[:, j * 512:(j + 1) * 512])

    q = proj(1)
    qb_ref[...] = (q * (1.0 / math.sqrt(HEAD_DIM))).astype(BF16).reshape(bb, tm, QK_WIDTH)
    k = proj(2)
    if feature_major_keys:
        assert bb == 1
        kt = k.T
        k_ref[0] = kt
        kb_ref[0] = kt.astype(BF16)
    else:
        k_ref[...] = k.reshape(bb, tm, QK_WIDTH)
        kb_ref[...] = k.astype(BF16).reshape(bb, tm, QK_WIDTH)
    v = proj(3)
    vb_ref[...] = v.astype(BF16).reshape(bb, tm, V_WIDTH)
    for h in range(N_HEADS):
        v_ref[:, pl.ds(h, tm, stride=N_HEADS), :] = v[:, h * V_DIM:(h + 1) * V_DIM].reshape(bb, tm, V_DIM)

    @pl.when(t == 0)
    def _():
        ext_ref[:, 0:TAIL, :] = buf_ref[...]

    ext_ref[:, TAIL:TAIL + tm, :] = proj(0).reshape(bb, tm, POOL_WIDTH)

    pos = pos0 + t * tm + lax.broadcasted_iota(jnp.int32, (1, tm, 1), 1)
    for g, w in enumerate(POOL_WINDOWS):
        lanes = slice(g * POOL_GROUP, (g + 1) * POOL_GROUP)
        u = ext_ref[:, TAIL:TAIL + tm, lanes]
        acc = u
        for j in range(1, w):
            acc = acc + ext_ref[:, TAIL - j:TAIL - j + tm, lanes]
        cnt = jnp.minimum(pos + 1, w).astype(F32)
        diff = (acc / cnt - u).reshape(rows, POOL_GROUP).astype(BF16)
        y = _dot(diff, w_pool_ref[g]) * pscale_ref[:, lanes]
        pool_ref[:, :, lanes] = y.astype(BF16).reshape(bb, tm, POOL_GROUP)

    tail = ext_ref[:, tm:tm + TAIL, :]
    tail_ref[...] = tail
    ext_ref[:, 0:TAIL, :] = tail


def _proj_pool(x, buf, g, w_in, w_pool, pscale, *, pos0, bb, tm, feature_major_keys):
    B, T, _ = x.shape
    grid = (B // bb, T // tm)
    row_spec = lambda width: pl.BlockSpec((bb, tm, width), lambda b, t: (b, t, 0))
    const = lambda shape: pl.BlockSpec(shape, lambda b, t: (0,) * len(shape), pipeline_mode=pl.Buffered(1))
    seq_spec = pl.BlockSpec((bb, TAIL, POOL_WIDTH), lambda b, t: (b, 0, 0))
    act = lambda dtype: jax.ShapeDtypeStruct((B, T, QK_WIDTH), dtype)
    if feature_major_keys:
        key_spec = pl.BlockSpec((bb, QK_WIDTH, tm), lambda b, t: (b, 0, t))
        key_shape = lambda dtype: jax.ShapeDtypeStruct((B, QK_WIDTH, T), dtype)
    else:
        key_spec, key_shape = row_spec(QK_WIDTH), act
    return pl.pallas_call(
        functools.partial(_proj_pool_kernel, pos0=pos0, feature_major_keys=feature_major_keys),
        grid=grid,
        in_specs=[row_spec(D_MODEL), const((1, D_MODEL)), const((D_MODEL, 4 * 512)),
                  const((len(POOL_WINDOWS), POOL_GROUP, POOL_GROUP)), const((1, POOL_WIDTH)), seq_spec],
        out_specs=[key_spec, pl.BlockSpec((bb, tm * N_HEADS, V_DIM), lambda b, t: (b, t, 0)),
                   row_spec(QK_WIDTH), key_spec, row_spec(V_WIDTH), row_spec(POOL_WIDTH), seq_spec],
        out_shape=[key_shape(F32), jax.ShapeDtypeStruct((B, T * N_HEADS, V_DIM), F32),
                   act(BF16), key_shape(BF16), act(BF16), act(BF16),
                   jax.ShapeDtypeStruct((B, TAIL, POOL_WIDTH), F32)],
        scratch_shapes=[pltpu.VMEM((bb, TAIL + tm, POOL_WIDTH), F32)],
        compiler_params=pltpu.CompilerParams(dimension_semantics=("arbitrary", "arbitrary"),
                                             vmem_limit_bytes=VMEM_LIMIT),
        name="proj_pool",
    )(x, g, w_in, w_pool, pscale, buf)


def _prompt_attn_kernel(lq1, lk1, lq2, lk2, g_ref, q_ref, kt_ref, v_ref, o_ref, *, lambda_init):
    blk = q_ref.shape[1]
    h = pl.program_id(1)
    i = pl.program_id(2)
    slope = _head_slope(jnp.full((1, 1), h, jnp.int32))
    lam = _lambda(lq1, lk1, lq2, lk2, lambda_init)

    q = q_ref[0]
    lane = lax.broadcasted_iota(jnp.int32, q.shape, 1)
    zero = jnp.zeros_like(q)
    q_half = (jnp.where(lane < HEAD_DIM, q, zero), jnp.where(lane >= HEAD_DIM, q, zero))

    r = lax.broadcasted_iota(jnp.int32, (blk, blk), 0)
    c = lax.broadcasted_iota(jnp.int32, (blk, blk), 1)
    visible = (c >> CHUNK_BITS) <= (r >> CHUNK_BITS)
    bias_diag = -slope * jnp.abs(r - c).astype(F32)
    bias_past = slope * (c - r).astype(F32)

    def block(j):
        start = pl.multiple_of(j * blk, blk)
        return kt_ref[0, :, pl.ds(start, blk)], v_ref[0, pl.ds(start, blk), :]

    kd, vd = block(i)
    state = []
    for half in range(2):
        s = jnp.where(visible, _dot(q_half[half], kd) + bias_diag, -jnp.inf)
        m = jnp.max(s, axis=-1, keepdims=True)
        p = jnp.exp(s - m)
        l = jnp.sum(p, axis=-1, keepdims=True)
        state += [m, l, _dot(p.astype(BF16), vd)]

    def body(j, st):
        kj, vj = block(j)
        shift = slope * ((j - i) * blk).astype(F32)
        new = []
        for half in range(2):
            m, l, acc = st[3 * half:3 * half + 3]
            s = _dot(q_half[half], kj) + bias_past
            m_new = jnp.maximum(m, jnp.max(s, axis=-1, keepdims=True) + shift)
            p = jnp.exp(s - (m_new - shift))
            alpha = jnp.exp(m - m_new)
            l = alpha * l + jnp.sum(p, axis=-1, keepdims=True)
            acc = alpha * acc + _dot(p.astype(BF16), vj)
            new += [m_new, l, acc]
        return tuple(new)

    _, l0, acc0, _, l1, acc1 = lax.fori_loop(0, i, body, tuple(state))
    o_ref[0] = _sub_ln(acc0 / l0, acc1 / l1, lam, g_ref[...], lambda_init).astype(o_ref.dtype)


def _prompt_attn(qb, ktb, vb, lams, subln_g, *, lambda_init, blk):
    B, T, _ = qb.shape
    grid = (B, N_HEADS, T // blk)
    vec = lambda n: pl.BlockSpec((1, n), lambda b, h, i: (0, 0))
    q_spec = pl.BlockSpec((1, blk, HEAD_LANES), lambda b, h, i: (b, i, h))
    kt_spec = pl.BlockSpec((1, HEAD_LANES, T), lambda b, h, i: (b, h, 0))
    v_spec = pl.BlockSpec((1, T, V_DIM), lambda b, h, i: (b, 0, h))
    return pl.pallas_call(
        functools.partial(_prompt_attn_kernel, lambda_init=lambda_init),
        grid=grid,
        in_specs=[vec(HEAD_DIM)] * 4 + [vec(V_DIM), q_spec, kt_spec, v_spec],
        out_specs=q_spec,
        out_shape=jax.ShapeDtypeStruct((B, T, V_WIDTH), BF16),
        compiler_params=pltpu.CompilerParams(dimension_semantics=("arbitrary",) * 3,
                                             vmem_limit_bytes=VMEM_LIMIT),
        name="prompt_attn",
    )(*lams, subln_g, qb, ktb, vb)


def _sample_attn_kernel(lq1, lk1, lq2, lk2, g_ref, q_ref, kn_ref, vn_ref, ckt_ref, cv_ref, o_ref,
                        qrows_ref, m_ref, l_ref, acc_ref, *, lambda_init, past_len):
    tq = q_ref.shape[1]
    head_rows = 2 * tq
    n_rows = N_HEADS * head_rows
    kb = ckt_ref.shape[2]
    j = pl.program_id(1)
    nj = pl.num_programs(1)

    tq_bits = tq.bit_length() - 1
    assert tq == 1 << tq_bits
    row = lax.broadcasted_iota(jnp.int32, (n_rows, 1), 0)
    q_pos = past_len + (row & (tq - 1))
    slope = _head_slope(row >> (tq_bits + 1))

    @pl.when(j == 0)
    def _():
        q = q_ref[0]
        tiled = jnp.concatenate([q] * (2 * N_HEADS), axis=0)
        r = lax.broadcasted_iota(jnp.int32, tiled.shape, 0)
        c = lax.broadcasted_iota(jnp.int32, tiled.shape, 1)
        qrows_ref[...] = jnp.where((r >> tq_bits) == (c >> HEAD_DIM_BITS), tiled, jnp.zeros_like(tiled))
        m_ref[...] = jnp.full(m_ref.shape, -jnp.inf, F32)
        l_ref[...] = jnp.zeros(l_ref.shape, F32)
        acc_ref[...] = jnp.zeros(acc_ref.shape, F32)

    def update(s, k_pos, head_values):
        s = s - slope * jnp.abs(q_pos - k_pos).astype(F32)
        m_old = m_ref[...]
        m_new = jnp.maximum(m_old, jnp.max(s, axis=-1, keepdims=True))
        p = jnp.exp(s - m_new)
        alpha = jnp.exp(m_old - m_new)
        l_ref[...] = alpha * l_ref[...] + jnp.sum(p, axis=-1, keepdims=True)
        m_ref[...] = m_new
        p = p.astype(BF16)
        for h in range(N_HEADS):
            rows = slice(h * head_rows, (h + 1) * head_rows)
            acc_ref[rows, :] = alpha[rows] * acc_ref[rows, :] + _dot(p[rows], head_values(h))

    update(_dot(qrows_ref[...], ckt_ref[0].astype(BF16)),
           j * kb + lax.broadcasted_iota(jnp.int32, (1, kb), 1),
           lambda h: cv_ref[0, pl.ds(h, kb, stride=N_HEADS), :].astype(BF16))

    @pl.when(j == nj - 1)
    def _():
        update(_nt_dot(qrows_ref[...], kn_ref[0]),
               past_len + lax.broadcasted_iota(jnp.int32, (1, tq), 1),
               lambda h: vn_ref[0, :, h * V_DIM:(h + 1) * V_DIM])
        lam = _lambda(lq1, lk1, lq2, lk2, lambda_init)
        out = acc_ref[...] / l_ref[...]
        for h in range(N_HEADS):
            o0 = out[h * head_rows:h * head_rows + tq]
            o1 = out[h * head_rows + tq:(h + 1) * head_rows]
            o_ref[0, :, h * V_DIM:(h + 1) * V_DIM] = _sub_ln(o0, o1, lam, g_ref[...], lambda_init).astype(o_ref.dtype)


def _sample_attn(qb, kb_new, vb_new, cache_kt, cache_v, lams, subln_g, *, lambda_init, kblk):
    B, tq, _ = qb.shape
    past_len = cache_kt.shape[2]
    n_rows = 2 * N_HEADS * tq
    grid = (B, past_len // kblk)
    vec = lambda n: pl.BlockSpec((1, n), lambda b, j: (0, 0))
    new_spec = pl.BlockSpec((1, tq, QK_WIDTH), lambda b, j: (b, 0, 0))
    kt_spec = pl.BlockSpec((1, QK_WIDTH, kblk), lambda b, j: (b, 0, j))
    v_spec = pl.BlockSpec((1, kblk * N_HEADS, V_DIM), lambda b, j: (b, j, 0))
    return pl.pallas_call(
        functools.partial(_sample_attn_kernel, lambda_init=lambda_init, past_len=past_len),
        grid=grid,
        in_specs=[vec(HEAD_DIM)] * 4 + [vec(V_DIM), new_spec, new_spec, new_spec, kt_spec, v_spec],
        out_specs=new_spec,
        out_shape=jax.ShapeDtypeStruct((B, tq, V_WIDTH), BF16),
        scratch_shapes=[pltpu.VMEM((n_rows, QK_WIDTH), BF16), pltpu.VMEM((n_rows, 1), F32),
                        pltpu.VMEM((n_rows, 1), F32), pltpu.VMEM((n_rows, V_DIM), F32)],
        compiler_params=pltpu.CompilerParams(dimension_semantics=("arbitrary", "arbitrary"),
                                             vmem_limit_bytes=VMEM_LIMIT),
        name="sample_attn",
    )(*lams, subln_g, qb, kb_new, vb_new, cache_kt, cache_v)


def _out_mlp_kernel(pool_ref, attn_ref, x_ref, w_out_ref, gf_ref, w_up_ref, w_down_ref, gl_ref, y_ref):
    mixed = (_dot(pool_ref[...], w_out_ref[0:POOL_WIDTH, :]) + _dot(attn_ref[...], w_out_ref[POOL_WIDTH:, :]))
    h = x_ref[...] + mixed
    ms = jnp.mean(h * h, axis=-1, keepdims=True)
    hn = (h * lax.rsqrt(ms + EPS) * gf_ref[...]).astype(BF16)
    y = h
    for c in range(D_FF // FF_CHUNK):
        cols = slice(c * FF_CHUNK, (c + 1) * FF_CHUNK)
        a = jnp.maximum(_dot(hn, w_up_ref[:, cols]), 0.0)
        y = y + _dot((a * a).astype(BF16), w_down_ref[cols, :])
    ms = jnp.mean(y * y, axis=-1, keepdims=True)
    y_ref[...] = y * lax.rsqrt(ms + EPS) * gl_ref[...]


def _out_mlp(pool, attn, x, w_out, gf, w_up, w_down, gl, *, tm):
    rows = x.shape[0]
    row_spec = lambda width: pl.BlockSpec((tm, width), lambda r: (r, 0))
    const = lambda shape: pl.BlockSpec(shape, lambda r: (0, 0), pipeline_mode=pl.Buffered(1))
    return pl.pallas_call(
        _out_mlp_kernel,
        grid=(rows // tm,),
        in_specs=[row_spec(POOL_WIDTH), row_spec(V_WIDTH), row_spec(D_MODEL),
                  const((D_MODEL, D_MODEL)), const((1, D_MODEL)), const((D_MODEL, D_FF)),
                  const((D_FF, D_MODEL)), const((1, D_MODEL))],
        out_specs=row_spec(D_MODEL),
        out_shape=jax.ShapeDtypeStruct((rows, D_MODEL), F32),
        compiler_params=pltpu.CompilerParams(dimension_semantics=("arbitrary",),
                                             vmem_limit_bytes=VMEM_LIMIT),
        name="out_mlp",
    )(pool, attn, x, w_out, gf, w_up, w_down, gl)


def _keys_from_feature_major(kt, frames):
    streams = kt.shape[0]
    return jnp.transpose(kt.reshape(streams, N_HEADS, 2, HEAD_DIM, frames), (0, 4, 1, 2, 3))[None]


def kernel(x_prompt, x_sample, state_pool, cache_k, cache_v, norm_mix_g, w_in, w_pool, pool_scale,
           lambda_q1, lambda_k1, lambda_q2, lambda_k2, subln_g, w_out, norm_ffn_g, w_up, w_down,
           norm_final_g):
    assert w_in.shape[0] == 1, "one layer per call"
    B, T, _ = x_prompt.shape
    S, TS, _ = x_sample.shape
    past_len = cache_k.shape[2]
    lambda_init = 0.8 - 0.6 * math.exp(-0.3 * 0)

    g_mix = norm_mix_g[0][None]
    g_ffn = norm_ffn_g[0][None]
    g_fin = norm_final_g[None]
    g_sub = subln_g[0][None]
    pscale = pool_scale[0][None]
    lams = (lambda_q1[0][None], lambda_k1[0][None], lambda_q2[0][None], lambda_k2[0][None])
    w_in_b = w_in[0].astype(BF16)
    w_pool_b = w_pool[0].astype(BF16)
    w_out_b = w_out[0].astype(BF16)
    w_up_b = w_up[0].astype(BF16)
    w_down_b = w_down[0].astype(BF16)

    zero_buf = jnp.zeros((B, TAIL, POOL_WIDTH), F32)
    kt_p, v_p, qb, ktb, vb, pool_p, tail_p = _proj_pool(
        x_prompt, zero_buf, g_mix, w_in_b, w_pool_b, pscale, pos0=0, bb=1, tm=PROMPT_ROWS,
        feature_major_keys=True)
    attn_p = _prompt_attn(qb, ktb, vb, lams, g_sub, lambda_init=lambda_init, blk=ATTN_BLOCK)
    y_p = _out_mlp(pool_p.reshape(B * T, -1), attn_p.reshape(B * T, -1), x_prompt.reshape(B * T, -1),
                   w_out_b, g_ffn, w_up_b, w_down_b, g_fin, tm=PROMPT_ROWS).reshape(B, T, D_MODEL)

    buf_s = jnp.pad(state_pool[0], ((0, 0), (TAIL - POOL_STATE, 0), (0, 0)))
    k_s, v_s, qb_s, kb_s, vb_s, pool_s, tail_s = _proj_pool(
        x_sample, buf_s, g_mix, w_in_b, w_pool_b, pscale, pos0=past_len, bb=S, tm=TS,
        feature_major_keys=False)
    cache_kt = jnp.transpose(cache_k[0], (0, 2, 3, 4, 1)).reshape(S, QK_WIDTH, past_len)
    cache_vi = cache_v[0].reshape(S, past_len * N_HEADS, V_DIM)
    attn_s = _sample_attn(qb_s, kb_s, vb_s, cache_kt, cache_vi, lams, g_sub,
                          lambda_init=lambda_init, kblk=CACHE_BLOCK)
    y_s = _out_mlp(pool_s.reshape(S * TS, -1), attn_s.reshape(S * TS, -1), x_sample.reshape(S * TS, -1),
                   w_out_b, g_ffn, w_up_b, w_down_b, g_fin, tm=S * TS).reshape(S, TS, D_MODEL)

    return (y_p, y_s,
            tail_p[:, TAIL - POOL_STATE:][None],
            _keys_from_feature_major(kt_p, T),
            v_p.reshape(1, B, T, N_HEADS, V_DIM),
            tail_s[:, TAIL - POOL_STATE:][None],
            k_s.reshape(1, S, TS, N_HEADS, 2, HEAD_DIM),
            v_s.reshape(1, S, TS, N_HEADS, V_DIM))
```

```python
import functools
import math

import jax
import jax.numpy as jnp
from jax import lax
from jax.experimental import pallas as pl
from jax.experimental.pallas import tpu as pltpu

D_MODEL = 1024
POOL_WIDTH = 512
POOL_WINDOWS = (2, 4, 8, 16)
POOL_GROUP = 128
POOL_STATE = 15
N_HEADS = 4
HEAD_DIM = 64
V_DIM = 128
QK_WIDTH = 512
V_WIDTH = N_HEADS * V_DIM
D_FF = 4096
CHUNK_BITS = 6
HEAD_DIM_BITS = 6
assert HEAD_DIM == 1 << HEAD_DIM_BITS
EPS = 1e-6
SUBLN_EPS = 1e-5

TAIL = 16
HEAD_LANES = 2 * HEAD_DIM
SUM_ROWS = 16
VT_ROWS = V_DIM + SUM_ROWS

BF16 = jnp.bfloat16
F32 = jnp.float32

PROMPT_ROWS = 512
ATTN_BLOCK = 256
CACHE_BLOCK = 2048
FF_CHUNK = 1024
VMEM_LIMIT = 56 * 1024 * 1024


def _nt_dot(a, b):
    return lax.dot_general(a, b, (((1,), (1,)), ((), ())), preferred_element_type=F32)


def _dot(a, b):
    return jnp.dot(a, b, preferred_element_type=F32)


def _head_slope(h):
    return lax.bitcast_convert_type((127 - 2 * (h + 1)) << 23, F32)


def _lambda(lq1, lk1, lq2, lk2, lambda_init):
    return (jnp.exp(jnp.sum(lq1[...] * lk1[...], axis=-1, keepdims=True))
            - jnp.exp(jnp.sum(lq2[...] * lk2[...], axis=-1, keepdims=True)) + lambda_init)


def _sub_ln(o0, o1, lam, g, lambda_init):
    o = o0 - lam * o1
    ms = jnp.mean(o * o, axis=-1, keepdims=True)
    return (o * lax.rsqrt(ms + SUBLN_EPS) * g) * (1.0 - lambda_init)


def _proj_pool_kernel(x_ref, g_ref, w_in_ref, w_pool_ref, pscale_ref, buf_ref,
                      k_ref, v_ref, qb_ref, kb_ref, vb_ref, pool_ref, tail_ref, ext_ref,
                      *, pos0, feature_major):
    bb, tm, _ = x_ref.shape
    rows = bb * tm
    t = pl.program_id(1)

    x = x_ref[...].reshape(rows, D_MODEL)
    ms = jnp.mean(x * x, axis=-1, keepdims=True)
    hn = (x * lax.rsqrt(ms + EPS) * g_ref[...]).astype(BF16)

    def proj(j):
        return _dot(hn, w_in_ref[:, j * 512:(j + 1) * 512])

    q = proj(1)
    qb_ref[...] = (q * (1.0 / math.sqrt(HEAD_DIM))).astype(BF16).reshape(bb, tm, QK_WIDTH)
    k = proj(2)
    kb_ref[...] = k.astype(BF16).reshape(bb, tm, QK_WIDTH)
    v = proj(3)
    if feature_major:
        assert bb == 1
        k_ref[0] = k.T
        vt = v.T.astype(BF16)
        for h in range(N_HEADS):
            vb_ref[0, h * VT_ROWS:h * VT_ROWS + V_DIM, :] = vt[h * V_DIM:(h + 1) * V_DIM]
            vb_ref[0, h * VT_ROWS + V_DIM:(h + 1) * VT_ROWS, :] = jnp.ones((SUM_ROWS, tm), BF16)
    else:
        k_ref[...] = k.reshape(bb, tm, QK_WIDTH)
        vb_ref[...] = v.astype(BF16).reshape(bb, tm, V_WIDTH)
    for h in range(N_HEADS):
        v_ref[:, pl.ds(h, tm, stride=N_HEADS), :] = v[:, h * V_DIM:(h + 1) * V_DIM].reshape(bb, tm, V_DIM)

    @pl.when(t == 0)
    def _():
        ext_ref[:, 0:TAIL, :] = buf_ref[...]

    ext_ref[:, TAIL:TAIL + tm, :] = proj(0).reshape(bb, tm, POOL_WIDTH)

    pos = pos0 + t * tm + lax.broadcasted_iota(jnp.int32, (1, tm, 1), 1)
    for g, w in enumerate(POOL_WINDOWS):
        lanes = slice(g * POOL_GROUP, (g + 1) * POOL_GROUP)
        u = ext_ref[:, TAIL:TAIL + tm, lanes]
        acc = u
        for j in range(1, w):
            acc = acc + ext_ref[:, TAIL - j:TAIL - j + tm, lanes]
        cnt = jnp.minimum(pos + 1, w).astype(F32)
        diff = (acc / cnt - u).reshape(rows, POOL_GROUP).astype(BF16)
        y = _dot(diff, w_pool_ref[g]) * pscale_ref[:, lanes]
        pool_ref[:, :, lanes] = y.astype(BF16).reshape(bb, tm, POOL_GROUP)

    tail = ext_ref[:, tm:tm + TAIL, :]
    tail_ref[...] = tail
    ext_ref[:, 0:TAIL, :] = tail


def _proj_pool(x, buf, g, w_in, w_pool, pscale, *, pos0, bb, tm, feature_major):
    B, T, _ = x.shape
    grid = (B // bb, T // tm)
    row_spec = lambda width: pl.BlockSpec((bb, tm, width), lambda b, t: (b, t, 0))
    const = lambda shape: pl.BlockSpec(shape, lambda b, t: (0,) * len(shape), pipeline_mode=pl.Buffered(1))
    seq_spec = pl.BlockSpec((bb, TAIL, POOL_WIDTH), lambda b, t: (b, 0, 0))
    act = lambda dtype: jax.ShapeDtypeStruct((B, T, QK_WIDTH), dtype)
    if feature_major:
        fm_spec = lambda rows: pl.BlockSpec((bb, rows, tm), lambda b, t: (b, 0, t))
        fm_shape = lambda rows, dtype: jax.ShapeDtypeStruct((B, rows, T), dtype)
        k_spec, k_shape = fm_spec(QK_WIDTH), fm_shape(QK_WIDTH, F32)
        vb_spec, vb_shape = fm_spec(N_HEADS * VT_ROWS), fm_shape(N_HEADS * VT_ROWS, BF16)
    else:
        k_spec, k_shape = row_spec(QK_WIDTH), act(F32)
        vb_spec, vb_shape = row_spec(V_WIDTH), act(BF16)
    return pl.pallas_call(
        functools.partial(_proj_pool_kernel, pos0=pos0, feature_major=feature_major),
        grid=grid,
        in_specs=[row_spec(D_MODEL), const((1, D_MODEL)), const((D_MODEL, 4 * 512)),
                  const((len(POOL_WINDOWS), POOL_GROUP, POOL_GROUP)), const((1, POOL_WIDTH)), seq_spec],
        out_specs=[k_spec, pl.BlockSpec((bb, tm * N_HEADS, V_DIM), lambda b, t: (b, t, 0)),
                   row_spec(QK_WIDTH), row_spec(QK_WIDTH), vb_spec, row_spec(POOL_WIDTH), seq_spec],
        out_shape=[k_shape, jax.ShapeDtypeStruct((B, T * N_HEADS, V_DIM), F32),
                   act(BF16), act(BF16), vb_shape, act(BF16),
                   jax.ShapeDtypeStruct((B, TAIL, POOL_WIDTH), F32)],
        scratch_shapes=[pltpu.VMEM((bb, TAIL + tm, POOL_WIDTH), F32)],
        compiler_params=pltpu.CompilerParams(dimension_semantics=("arbitrary", "arbitrary"),
                                             vmem_limit_bytes=VMEM_LIMIT),
        name="proj_pool",
    )(x, g, w_in, w_pool, pscale, buf)


def _prompt_attn_kernel(lq1, lk1, lq2, lk2, gcol_ref, q_ref, k_ref, vt_ref, o_ref,
                        bias_past_ref, bias_diag_ref, s_ref, *, lambda_init, blk):
    n_blocks = q_ref.shape[1] // blk
    h = pl.program_id(0)
    slope = _head_slope(jnp.full((1, 1), h, jnp.int32))

    @pl.when(pl.program_id(1) == 0)
    def _():
        kg = lax.broadcasted_iota(jnp.int32, bias_past_ref.shape, 0)
        qq = lax.broadcasted_iota(jnp.int32, bias_past_ref.shape, 1) & (blk - 1)
        bias_past_ref[...] = slope * (kg - qq).astype(F32)
        kk = lax.broadcasted_iota(jnp.int32, bias_diag_ref.shape, 0)
        qq = lax.broadcasted_iota(jnp.int32, bias_diag_ref.shape, 1) & (blk - 1)
        visible = (kk >> CHUNK_BITS) <= (qq >> CHUNK_BITS)
        bias_diag_ref[...] = jnp.where(visible, -slope * jnp.abs(qq - kk).astype(F32), -jnp.inf)

    lam = _lambda(lq1, lk1, lq2, lk2, lambda_init)
    feat = lax.broadcasted_iota(jnp.int32, (HEAD_LANES, blk), 0)
    zero = jnp.zeros((HEAD_LANES, blk), BF16)

    def scores(c):
        n = c * blk
        qt = q_ref[0, n:n + blk, :].astype(F32).T.astype(BF16)
        q_cols = jnp.concatenate([jnp.where(feat < HEAD_DIM, qt, zero), jnp.where(feat >= HEAD_DIM, qt, zero)],
                                 axis=1)
        def sublane_max(s):
            return jnp.max(s.reshape(blk // 8, 8, s.shape[1]), axis=0)

        s_diag = _dot(k_ref[0, n:n + blk, :], q_cols) + bias_diag_ref[...]
        s_ref[c % 2, n:n + blk, :] = s_diag
        m = jnp.max(sublane_max(s_diag), axis=0, keepdims=True)
        if c > 0:
            m_past = None
            for j in range(c):
                rows = slice(j * blk, (j + 1) * blk)
                s_past = _dot(k_ref[0, rows, :], q_cols) + bias_past_ref[rows, :]
                s_ref[c % 2, rows, :] = s_past
                m_past = sublane_max(s_past) if j == 0 else jnp.maximum(m_past, sublane_max(s_past))
            m = jnp.maximum(m, jnp.max(m_past, axis=0, keepdims=True) - slope * float(n))
        return m

    def outputs(c, m):
        n = c * blk
        acc = _dot(vt_ref[0, :, n:n + blk], jnp.exp(s_ref[c % 2, n:n + blk, :] - m).astype(BF16))
        m_past = m + slope * float(n)
        for j in range(c):
            rows = slice(j * blk, (j + 1) * blk)
            acc = acc + _dot(vt_ref[0, :, rows], jnp.exp(s_ref[c % 2, rows, :] - m_past).astype(BF16))
        o = [acc[0:V_DIM, half * blk:(half + 1) * blk] / acc[V_DIM:V_DIM + 1, half * blk:(half + 1) * blk]
             for half in range(2)]
        ot = o[0] - lam * o[1]
        ms = jnp.mean(ot * ot, axis=0, keepdims=True)
        yt = (ot * lax.rsqrt(ms + SUBLN_EPS) * gcol_ref[...]) * (1.0 - lambda_init)
        o_ref[0, n:n + blk, :] = yt.T.astype(o_ref.dtype)

    m_next = scores(0)
    for c in range(n_blocks):
        m = m_next
        if c + 1 < n_blocks:
            m_next = scores(c + 1)
        outputs(c, m)


def _prompt_attn(qb, kb, vtb, lams, subln_gcol, *, lambda_init, blk):
    B, T, _ = qb.shape
    grid = (N_HEADS, B)
    vec = lambda n: pl.BlockSpec((1, n), lambda h, b: (0, 0))
    qk_spec = pl.BlockSpec((1, T, HEAD_LANES), lambda h, b: (b, 0, h))
    vt_spec = pl.BlockSpec((1, VT_ROWS, T), lambda h, b: (b, h, 0))
    return pl.pallas_call(
        functools.partial(_prompt_attn_kernel, lambda_init=lambda_init, blk=blk),
        grid=grid,
        in_specs=[vec(HEAD_DIM)] * 4 + [pl.BlockSpec((V_DIM, 1), lambda h, b: (0, 0)), qk_spec, qk_spec, vt_spec],
        out_specs=qk_spec,
        out_shape=jax.ShapeDtypeStruct((B, T, V_WIDTH), BF16),
        scratch_shapes=[pltpu.VMEM((T - blk, 2 * blk), F32), pltpu.VMEM((blk, 2 * blk), F32),
                        pltpu.VMEM((2, T, 2 * blk), F32)],
        compiler_params=pltpu.CompilerParams(dimension_semantics=("arbitrary",) * 2,
                                             vmem_limit_bytes=VMEM_LIMIT),
        name="prompt_attn",
    )(*lams, subln_gcol, qb, kb, vtb)


def _sample_attn_kernel(lq1, lk1, lq2, lk2, g_ref, q_ref, kn_ref, vn_ref, ckt_ref, cv_ref, o_ref,
                        qrows_ref, m_ref, l_ref, acc_ref, *, lambda_init, past_len):
    tq = q_ref.shape[1]
    head_rows = 2 * tq
    n_rows = N_HEADS * head_rows
    kb = ckt_ref.shape[2]
    j = pl.program_id(1)
    nj = pl.num_programs(1)

    tq_bits = tq.bit_length() - 1
    assert tq == 1 << tq_bits
    row = lax.broadcasted_iota(jnp.int32, (n_rows, 1), 0)
    q_pos = past_len + (row & (tq - 1))
    slope = _head_slope(row >> (tq_bits + 1))

    @pl.when(j == 0)
    def _():
        q = q_ref[0]
        tiled = jnp.concatenate([q] * (2 * N_HEADS), axis=0)
        r = lax.broadcasted_iota(jnp.int32, tiled.shape, 0)
        c = lax.broadcasted_iota(jnp.int32, tiled.shape, 1)
        qrows_ref[...] = jnp.where((r >> tq_bits) == (c >> HEAD_DIM_BITS), tiled, jnp.zeros_like(tiled))
        m_ref[...] = jnp.full(m_ref.shape, -jnp.inf, F32)
        l_ref[...] = jnp.zeros(l_ref.shape, F32)
        acc_ref[...] = jnp.zeros(acc_ref.shape, F32)

    def update(s, k_pos, head_values):
        s = s - slope * jnp.abs(q_pos - k_pos).astype(F32)
        m_old = m_ref[...]
        m_new = jnp.maximum(m_old, jnp.max(s, axis=-1, keepdims=True))
        p = jnp.exp(s - m_new)
        alpha = jnp.exp(m_old - m_new)
        l_ref[...] = alpha * l_ref[...] + jnp.sum(p, axis=-1, keepdims=True)
        m_ref[...] = m_new
        p = p.astype(BF16)
        for h in range(N_HEADS):
            rows = slice(h * head_rows, (h + 1) * head_rows)
            acc_ref[rows, :] = alpha[rows] * acc_ref[rows, :] + _dot(p[rows], head_values(h))

    update(_dot(qrows_ref[...], ckt_ref[0].astype(BF16)),
           j * kb + lax.broadcasted_iota(jnp.int32, (1, kb), 1),
           lambda h: cv_ref[0, pl.ds(h, kb, stride=N_HEADS), :].astype(BF16))

    @pl.when(j == nj - 1)
    def _():
        update(_nt_dot(qrows_ref[...], kn_ref[0]),
               past_len + lax.broadcasted_iota(jnp.int32, (1, tq), 1),
               lambda h: vn_ref[0, :, h * V_DIM:(h + 1) * V_DIM])
        lam = _lambda(lq1, lk1, lq2, lk2, lambda_init)
        out = acc_ref[...] / l_ref[...]
        for h in range(N_HEADS):
            o0 = out[h * head_rows:h * head_rows + tq]
            o1 = out[h * head_rows + tq:(h + 1) * head_rows]
            o_ref[0, :, h * V_DIM:(h + 1) * V_DIM] = _sub_ln(o0, o1, lam, g_ref[...], lambda_init).astype(o_ref.dtype)


def _sample_attn(qb, kb_new, vb_new, cache_kt, cache_v, lams, subln_g, *, lambda_init, kblk):
    B, tq, _ = qb.shape
    past_len = cache_kt.shape[2]
    n_rows = 2 * N_HEADS * tq
    grid = (B, past_len // kblk)
    vec = lambda n: pl.BlockSpec((1, n), lambda b, j: (0, 0))
    new_spec = pl.BlockSpec((1, tq, QK_WIDTH), lambda b, j: (b, 0, 0))
    kt_spec = pl.BlockSpec((1, QK_WIDTH, kblk), lambda b, j: (b, 0, j))
    v_spec = pl.BlockSpec((1, kblk * N_HEADS, V_DIM), lambda b, j: (b, j, 0))
    return pl.pallas_call(
        functools.partial(_sample_attn_kernel, lambda_init=lambda_init, past_len=past_len),
        grid=grid,
        in_specs=[vec(HEAD_DIM)] * 4 + [vec(V_DIM), new_spec, new_spec, new_spec, kt_spec, v_spec],
        out_specs=new_spec,
        out_shape=jax.ShapeDtypeStruct((B, tq, V_WIDTH), BF16),
        scratch_shapes=[pltpu.VMEM((n_rows, QK_WIDTH), BF16), pltpu.VMEM((n_rows, 1), F32),
                        pltpu.VMEM((n_rows, 1), F32), pltpu.VMEM((n_rows, V_DIM), F32)],
        compiler_params=pltpu.CompilerParams(dimension_semantics=("arbitrary", "arbitrary"),
                                             vmem_limit_bytes=VMEM_LIMIT),
        name="sample_attn",
    )(*lams, subln_g, qb, kb_new, vb_new, cache_kt, cache_v)


def _out_mlp_kernel(pool_ref, attn_ref, x_ref, w_out_ref, gf_ref, w_up_ref, w_down_ref, gl_ref, y_ref):
    mixed = (_dot(pool_ref[...], w_out_ref[0:POOL_WIDTH, :]) + _dot(attn_ref[...], w_out_ref[POOL_WIDTH:, :]))
    h = x_ref[...] + mixed
    ms = jnp.mean(h * h, axis=-1, keepdims=True)
    hn = (h * lax.rsqrt(ms + EPS) * gf_ref[...]).astype(BF16)
    y = h
    for c in range(D_FF // FF_CHUNK):
        cols = slice(c * FF_CHUNK, (c + 1) * FF_CHUNK)
        a = jnp.maximum(_dot(hn, w_up_ref[:, cols]), 0.0)
        y = y + _dot((a * a).astype(BF16), w_down_ref[cols, :])
    ms = jnp.mean(y * y, axis=-1, keepdims=True)
    y_ref[...] = y * lax.rsqrt(ms + EPS) * gl_ref[...]


def _out_mlp(pool, attn, x, w_out, gf, w_up, w_down, gl, *, tm):
    rows = x.shape[0]
    row_spec = lambda width: pl.BlockSpec((tm, width), lambda r: (r, 0))
    const = lambda shape: pl.BlockSpec(shape, lambda r: (0, 0), pipeline_mode=pl.Buffered(1))
    return pl.pallas_call(
        _out_mlp_kernel,
        grid=(rows // tm,),
        in_specs=[row_spec(POOL_WIDTH), row_spec(V_WIDTH), row_spec(D_MODEL),
                  const((D_MODEL, D_MODEL)), const((1, D_MODEL)), const((D_MODEL, D_FF)),
                  const((D_FF, D_MODEL)), const((1, D_MODEL))],
        out_specs=row_spec(D_MODEL),
        out_shape=jax.ShapeDtypeStruct((rows, D_MODEL), F32),
        compiler_params=pltpu.CompilerParams(dimension_semantics=("arbitrary",),
                                             vmem_limit_bytes=VMEM_LIMIT),
        name="out_mlp",
    )(pool, attn, x, w_out, gf, w_up, w_down, gl)


def _keys_from_feature_major(kt, frames):
    streams = kt.shape[0]
    return jnp.transpose(kt.reshape(streams, N_HEADS, 2, HEAD_DIM, frames), (0, 4, 1, 2, 3))[None]


def kernel(x_prompt, x_sample, state_pool, cache_k, cache_v, norm_mix_g, w_in, w_pool, pool_scale,
           lambda_q1, lambda_k1, lambda_q2, lambda_k2, subln_g, w_out, norm_ffn_g, w_up, w_down,
           norm_final_g):
    assert w_in.shape[0] == 1, "one layer per call"
    B, T, _ = x_prompt.shape
    S, TS, _ = x_sample.shape
    past_len = cache_k.shape[2]
    lambda_init = 0.8 - 0.6 * math.exp(-0.3 * 0)

    g_mix = norm_mix_g[0][None]
    g_ffn = norm_ffn_g[0][None]
    g_fin = norm_final_g[None]
    g_sub = subln_g[0][None]
    pscale = pool_scale[0][None]
    lams = (lambda_q1[0][None], lambda_k1[0][None], lambda_q2[0][None], lambda_k2[0][None])
    w_in_b = w_in[0].astype(BF16)
    w_pool_b = w_pool[0].astype(BF16)
    w_out_b = w_out[0].astype(BF16)
    w_up_b = w_up[0].astype(BF16)
    w_down_b = w_down[0].astype(BF16)

    zero_buf = jnp.zeros((B, TAIL, POOL_WIDTH), F32)
    kt_p, v_p, qb, kb, vtb, pool_p, tail_p = _proj_pool(
        x_prompt, zero_buf, g_mix, w_in_b, w_pool_b, pscale, pos0=0, bb=1, tm=PROMPT_ROWS,
        feature_major=True)
    attn_p = _prompt_attn(qb, kb, vtb, lams, g_sub.reshape(V_DIM, 1), lambda_init=lambda_init, blk=ATTN_BLOCK)
    y_p = _out_mlp(pool_p.reshape(B * T, -1), attn_p.reshape(B * T, -1), x_prompt.reshape(B * T, -1),
                   w_out_b, g_ffn, w_up_b, w_down_b, g_fin, tm=PROMPT_ROWS).reshape(B, T, D_MODEL)

    buf_s = jnp.pad(state_pool[0], ((0, 0), (TAIL - POOL_STATE, 0), (0, 0)))
    k_s, v_s, qb_s, kb_s, vb_s, pool_s, tail_s = _proj_pool(
        x_sample, buf_s, g_mix, w_in_b, w_pool_b, pscale, pos0=past_len, bb=S, tm=TS,
        feature_major=False)
    cache_kt = jnp.transpose(cache_k[0], (0, 2, 3, 4, 1)).reshape(S, QK_WIDTH, past_len)
    cache_vi = cache_v[0].reshape(S, past_len * N_HEADS, V_DIM)
    attn_s = _sample_attn(qb_s, kb_s, vb_s, cache_kt, cache_vi, lams, g_sub,
                          lambda_init=lambda_init, kblk=CACHE_BLOCK)
    y_s = _out_mlp(pool_s.reshape(S * TS, -1), attn_s.reshape(S * TS, -1), x_sample.reshape(S * TS, -1),
                   w_out_b, g_ffn, w_up_b, w_down_b, g_fin, tm=S * TS).reshape(S, TS, D_MODEL)

    return (y_p, y_s,
            tail_p[:, TAIL - POOL_STATE:][None],
            _keys_from_feature_major(kt_p, T),
            v_p.reshape(1, B, T, N_HEADS, V_DIM),
            tail_s[:, TAIL - POOL_STATE:][None],
            k_s.reshape(1, S, TS, N_HEADS, 2, HEAD_DIM),
            v_s.reshape(1, S, TS, N_HEADS, V_DIM))
```

```python
import functools
import math

import jax
import jax.numpy as jnp
from jax import lax
from jax.experimental import pallas as pl
from jax.experimental.pallas import tpu as pltpu

D_MODEL = 1024
POOL_WIDTH = 512
POOL_WINDOWS = (2, 4, 8, 16)
POOL_GROUP = 128
POOL_STATE = 15
N_HEADS = 4
HEAD_DIM = 64
V_DIM = 128
QK_WIDTH = 512
V_WIDTH = N_HEADS * V_DIM
D_FF = 4096
CHUNK_BITS = 6
HEAD_DIM_BITS = 6
assert HEAD_DIM == 1 << HEAD_DIM_BITS
EPS = 1e-6
SUBLN_EPS = 1e-5

TAIL = 16
HEAD_LANES = 2 * HEAD_DIM
SUM_ROWS = 16
VT_ROWS = V_DIM + SUM_ROWS

BF16 = jnp.bfloat16
F32 = jnp.float32
LOG2_E = math.log2(math.e)

PROMPT_ROWS = 512
ATTN_BLOCK = 256
CACHE_BLOCK = 4096
FF_CHUNK = 1024
VMEM_LIMIT = 56 * 1024 * 1024


def _nt_dot(a, b):
    return lax.dot_general(a, b, (((1,), (1,)), ((), ())), preferred_element_type=F32)


def _dot(a, b):
    return jnp.dot(a, b, preferred_element_type=F32)


def _head_slope(h):
    return lax.bitcast_convert_type((127 - 2 * (h + 1)) << 23, F32) * LOG2_E


def _lambda(lq1, lk1, lq2, lk2, lambda_init):
    return (jnp.exp(jnp.sum(lq1[...] * lk1[...], axis=-1, keepdims=True))
            - jnp.exp(jnp.sum(lq2[...] * lk2[...], axis=-1, keepdims=True)) + lambda_init)


def _sub_ln(o0, o1, lam, g, lambda_init):
    o = o0 - lam * o1
    ms = jnp.mean(o * o, axis=-1, keepdims=True)
    return (o * lax.rsqrt(ms + SUBLN_EPS) * g) * (1.0 - lambda_init)


def _proj_pool_kernel(x_ref, g_ref, w_in_ref, w_pool_ref, pscale_ref, buf_ref,
                      k_ref, v_ref, qb_ref, kb_ref, vb_ref, pool_ref, tail_ref, ext_ref,
                      *, pos0, feature_major):
    bb, tm, _ = x_ref.shape
    rows = bb * tm
    t = pl.program_id(1)

    x = x_ref[...].reshape(rows, D_MODEL)
    ms = jnp.mean(x * x, axis=-1, keepdims=True)
    hn = (x * lax.rsqrt(ms + EPS) * g_ref[...]).astype(BF16)

    def proj(j):
        return _dot(hn, w_in_ref[:, j * 512:(j + 1) * 512])

    @pl.when(t == 0)
    def _():
        ext_ref[:, 0:TAIL, :] = buf_ref[...]

    ext_ref[:, TAIL:TAIL + tm, :] = proj(0).reshape(bb, tm, POOL_WIDTH)
    pos = pos0 + t * tm + lax.broadcasted_iota(jnp.int32, (1, tm, 1), 1)

    def pool_group(g):
        w = POOL_WINDOWS[g]
        lanes = slice(g * POOL_GROUP, (g + 1) * POOL_GROUP)
        ext = ext_ref[:, :, lanes]
        acc, span = ext, 1
        while span < w:
            acc = acc + pltpu.roll(acc, span, axis=1)
            span *= 2
        cnt = jnp.minimum(pos + 1, w).astype(F32)
        diff = (acc[:, TAIL:, :] / cnt - ext[:, TAIL:, :]).reshape(rows, POOL_GROUP).astype(BF16)
        y = _dot(diff, w_pool_ref[g]) * pscale_ref[:, lanes]
        pool_ref[:, :, lanes] = y.astype(BF16).reshape(bb, tm, POOL_GROUP)

    q = proj(1)
    qb_ref[...] = (q * (LOG2_E / math.sqrt(HEAD_DIM))).astype(BF16).reshape(bb, tm, QK_WIDTH)
    pool_group(0)
    pool_group(1)
    k = proj(2)
    kb_ref[...] = k.astype(BF16).reshape(bb, tm, QK_WIDTH)
    if feature_major:
        assert bb == 1
        k_ref[0] = k.T
    else:
        k_ref[...] = k.reshape(bb, tm, QK_WIDTH)
    pool_group(2)
    pool_group(3)
    v = proj(3)
    if feature_major:
        vt = v.T.astype(BF16)
        for h in range(N_HEADS):
            vb_ref[0, h * VT_ROWS:h * VT_ROWS + V_DIM, :] = vt[h * V_DIM:(h + 1) * V_DIM]
            vb_ref[0, h * VT_ROWS + V_DIM:(h + 1) * VT_ROWS, :] = jnp.ones((SUM_ROWS, tm), BF16)
    else:
        vb_ref[...] = v.astype(BF16).reshape(bb, tm, V_WIDTH)
    for h in range(N_HEADS):
        v_ref[:, pl.ds(h, tm, stride=N_HEADS), :] = v[:, h * V_DIM:(h + 1) * V_DIM].reshape(bb, tm, V_DIM)

    tail = ext_ref[:, tm:tm + TAIL, :]
    tail_ref[...] = tail
    ext_ref[:, 0:TAIL, :] = tail


def _proj_pool(x, buf, g, w_in, w_pool, pscale, *, pos0, bb, tm, feature_major):
    B, T, _ = x.shape
    grid = (B // bb, T // tm)
    row_spec = lambda width: pl.BlockSpec((bb, tm, width), lambda b, t: (b, t, 0))
    const = lambda shape: pl.BlockSpec(shape, lambda b, t: (0,) * len(shape), pipeline_mode=pl.Buffered(1))
    seq_spec = pl.BlockSpec((bb, TAIL, POOL_WIDTH), lambda b, t: (b, 0, 0))
    act = lambda dtype: jax.ShapeDtypeStruct((B, T, QK_WIDTH), dtype)
    if feature_major:
        fm_spec = lambda rows: pl.BlockSpec((bb, rows, tm), lambda b, t: (b, 0, t))
        fm_shape = lambda rows, dtype: jax.ShapeDtypeStruct((B, rows, T), dtype)
        k_spec, k_shape = fm_spec(QK_WIDTH), fm_shape(QK_WIDTH, F32)
        vb_spec, vb_shape = fm_spec(N_HEADS * VT_ROWS), fm_shape(N_HEADS * VT_ROWS, BF16)
    else:
        k_spec, k_shape = row_spec(QK_WIDTH), act(F32)
        vb_spec, vb_shape = row_spec(V_WIDTH), act(BF16)
    return pl.pallas_call(
        functools.partial(_proj_pool_kernel, pos0=pos0, feature_major=feature_major),
        grid=grid,
        in_specs=[row_spec(D_MODEL), const((1, D_MODEL)), const((D_MODEL, 4 * 512)),
                  const((len(POOL_WINDOWS), POOL_GROUP, POOL_GROUP)), const((1, POOL_WIDTH)), seq_spec],
        out_specs=[k_spec, pl.BlockSpec((bb, tm * N_HEADS, V_DIM), lambda b, t: (b, t, 0)),
                   row_spec(QK_WIDTH), row_spec(QK_WIDTH), vb_spec, row_spec(POOL_WIDTH), seq_spec],
        out_shape=[k_shape, jax.ShapeDtypeStruct((B, T * N_HEADS, V_DIM), F32),
                   act(BF16), act(BF16), vb_shape, act(BF16),
                   jax.ShapeDtypeStruct((B, TAIL, POOL_WIDTH), F32)],
        scratch_shapes=[pltpu.VMEM((bb, TAIL + tm, POOL_WIDTH), F32)],
        compiler_params=pltpu.CompilerParams(dimension_semantics=("arbitrary", "arbitrary"),
                                             vmem_limit_bytes=VMEM_LIMIT),
        name="proj_pool",
    )(x, g, w_in, w_pool, pscale, buf)


def _prompt_attn_kernel(lq1, lk1, lq2, lk2, gcol_ref, q_ref, k_ref, vt_ref, o_ref,
                        bias_past_ref, bias_diag_ref, s_ref, *, lambda_init, blk):
    n_blocks = q_ref.shape[1] // blk
    h = pl.program_id(0)
    slope = _head_slope(jnp.full((1, 1), h, jnp.int32))

    @pl.when(pl.program_id(1) == 0)
    def _():
        kg = lax.broadcasted_iota(jnp.int32, bias_past_ref.shape, 0)
        qq = lax.broadcasted_iota(jnp.int32, bias_past_ref.shape, 1) & (blk - 1)
        bias_past_ref[...] = slope * (kg - qq).astype(F32)
        kk = lax.broadcasted_iota(jnp.int32, bias_diag_ref.shape, 0)
        qq = lax.broadcasted_iota(jnp.int32, bias_diag_ref.shape, 1) & (blk - 1)
        visible = (kk >> CHUNK_BITS) <= (qq >> CHUNK_BITS)
        bias_diag_ref[...] = jnp.where(visible, -slope * jnp.abs(qq - kk).astype(F32), -jnp.inf)

    lam = _lambda(lq1, lk1, lq2, lk2, lambda_init)
    feat = lax.broadcasted_iota(jnp.int32, (HEAD_LANES, blk), 0)
    zero = jnp.zeros((HEAD_LANES, blk), BF16)

    def scores(c):
        n = c * blk
        qt = q_ref[0, n:n + blk, :].astype(F32).T.astype(BF16)
        q_cols = jnp.concatenate([jnp.where(feat < HEAD_DIM, qt, zero), jnp.where(feat >= HEAD_DIM, qt, zero)],
                                 axis=1)
        def sublane_max(s):
            return jnp.max(s.reshape(blk // 8, 8, s.shape[1]), axis=0)

        s_diag = _dot(k_ref[0, n:n + blk, :], q_cols) + bias_diag_ref[...]
        s_ref[c % 2, n:n + blk, :] = s_diag
        m = jnp.max(sublane_max(s_diag), axis=0, keepdims=True)
        if c > 0:
            m_past = None
            for j in range(c):
                rows = slice(j * blk, (j + 1) * blk)
                s_past = _dot(k_ref[0, rows, :], q_cols) + bias_past_ref[rows, :]
                s_ref[c % 2, rows, :] = s_past
                m_past = sublane_max(s_past) if j == 0 else jnp.maximum(m_past, sublane_max(s_past))
            m = jnp.maximum(m, jnp.max(m_past, axis=0, keepdims=True) - slope * float(n))
        return m

    def outputs(c, m):
        n = c * blk
        acc = _dot(vt_ref[0, :, n:n + blk], jnp.exp2(s_ref[c % 2, n:n + blk, :] - m).astype(BF16))
        m_past = m + slope * float(n)
        for j in range(c):
            rows = slice(j * blk, (j + 1) * blk)
            acc = acc + _dot(vt_ref[0, :, rows], jnp.exp2(s_ref[c % 2, rows, :] - m_past).astype(BF16))
        o = [acc[0:V_DIM, half * blk:(half + 1) * blk] / acc[V_DIM:V_DIM + 1, half * blk:(half + 1) * blk]
             for half in range(2)]
        ot = o[0] - lam * o[1]
        ms = jnp.mean(ot * ot, axis=0, keepdims=True)
        yt = (ot * lax.rsqrt(ms + SUBLN_EPS) * gcol_ref[...]) * (1.0 - lambda_init)
        o_ref[0, n:n + blk, :] = yt.T.astype(o_ref.dtype)

    m_next = scores(0)
    for c in range(n_blocks):
        m = m_next
        if c + 1 < n_blocks:
            m_next = scores(c + 1)
        outputs(c, m)


def _prompt_attn(qb, kb, vtb, lams, subln_gcol, *, lambda_init, blk):
    B, T, _ = qb.shape
    grid = (N_HEADS, B)
    vec = lambda n: pl.BlockSpec((1, n), lambda h, b: (0, 0))
    qk_spec = pl.BlockSpec((1, T, HEAD_LANES), lambda h, b: (b, 0, h))
    vt_spec = pl.BlockSpec((1, VT_ROWS, T), lambda h, b: (b, h, 0))
    return pl.pallas_call(
        functools.partial(_prompt_attn_kernel, lambda_init=lambda_init, blk=blk),
        grid=grid,
        in_specs=[vec(HEAD_DIM)] * 4 + [pl.BlockSpec((V_DIM, 1), lambda h, b: (0, 0)), qk_spec, qk_spec, vt_spec],
        out_specs=qk_spec,
        out_shape=jax.ShapeDtypeStruct((B, T, V_WIDTH), BF16),
        scratch_shapes=[pltpu.VMEM((T - blk, 2 * blk), F32), pltpu.VMEM((blk, 2 * blk), F32),
                        pltpu.VMEM((2, T, 2 * blk), F32)],
        compiler_params=pltpu.CompilerParams(dimension_semantics=("arbitrary",) * 2,
                                             vmem_limit_bytes=VMEM_LIMIT),
        name="prompt_attn",
    )(*lams, subln_gcol, qb, kb, vtb)


def _sample_attn_kernel(lq1, lk1, lq2, lk2, g_ref, q_ref, kn_ref, vn_ref, ckt_ref, cv_ref, o_ref,
                        qrows_ref, m_ref, l_ref, acc_ref, *, lambda_init, past_len):
    tq = q_ref.shape[1]
    head_rows = 2 * tq
    n_rows = N_HEADS * head_rows
    kb = ckt_ref.shape[2]
    j = pl.program_id(1)
    nj = pl.num_programs(1)

    tq_bits = tq.bit_length() - 1
    assert tq == 1 << tq_bits
    row = lax.broadcasted_iota(jnp.int32, (n_rows, 1), 0)
    q_pos = past_len + (row & (tq - 1))
    slope = _head_slope(row >> (tq_bits + 1))

    @pl.when(j == 0)
    def _():
        q = q_ref[0]
        tiled = jnp.concatenate([q] * (2 * N_HEADS), axis=0)
        r = lax.broadcasted_iota(jnp.int32, tiled.shape, 0)
        c = lax.broadcasted_iota(jnp.int32, tiled.shape, 1)
        qrows_ref[...] = jnp.where((r >> tq_bits) == (c >> HEAD_DIM_BITS), tiled, jnp.zeros_like(tiled))
        m_ref[...] = jnp.full(m_ref.shape, -jnp.inf, F32)
        l_ref[...] = jnp.zeros(l_ref.shape, F32)
        acc_ref[...] = jnp.zeros(acc_ref.shape, F32)

    def update(s, k_pos, head_values):
        s = s - slope * jnp.abs(q_pos - k_pos).astype(F32)
        m_old = m_ref[...]
        m_new = jnp.maximum(m_old, jnp.max(s, axis=-1, keepdims=True))
        p = jnp.exp2(s - m_new)
        alpha = jnp.exp2(m_old - m_new)
        l_ref[...] = alpha * l_ref[...] + jnp.sum(p, axis=-1, keepdims=True)
        m_ref[...] = m_new
        p = p.astype(BF16)
        for h in range(N_HEADS):
            rows = slice(h * head_rows, (h + 1) * head_rows)
            acc_ref[rows, :] = alpha[rows] * acc_ref[rows, :] + _dot(p[rows], head_values(h))

    update(_dot(qrows_ref[...], ckt_ref[0].astype(BF16)),
           j * kb + lax.broadcasted_iota(jnp.int32, (1, kb), 1),
           lambda h: cv_ref[0, pl.ds(h, kb, stride=N_HEADS), :].astype(BF16))

    @pl.when(j == nj - 1)
    def _():
        update(_nt_dot(qrows_ref[...], kn_ref[0]),
               past_len + lax.broadcasted_iota(jnp.int32, (1, tq), 1),
               lambda h: vn_ref[0, :, h * V_DIM:(h + 1) * V_DIM])
        lam = _lambda(lq1, lk1, lq2, lk2, lambda_init)
        out = acc_ref[...] / l_ref[...]
        for h in range(N_HEADS):
            o0 = out[h * head_rows:h * head_rows + tq]
            o1 = out[h * head_rows + tq:(h + 1) * head_rows]
            o_ref[0, :, h * V_DIM:(h + 1) * V_DIM] = _sub_ln(o0, o1, lam, g_ref[...], lambda_init).astype(o_ref.dtype)


def _sample_attn(qb, kb_new, vb_new, cache_kt, cache_v, lams, subln_g, *, lambda_init, kblk):
    B, tq, _ = qb.shape
    past_len = cache_kt.shape[2]
    n_rows = 2 * N_HEADS * tq
    grid = (B, past_len // kblk)
    vec = lambda n: pl.BlockSpec((1, n), lambda b, j: (0, 0))
    new_spec = pl.BlockSpec((1, tq, QK_WIDTH), lambda b, j: (b, 0, 0))
    kt_spec = pl.BlockSpec((1, QK_WIDTH, kblk), lambda b, j: (b, 0, j))
    v_spec = pl.BlockSpec((1, kblk * N_HEADS, V_DIM), lambda b, j: (b, j, 0))
    return pl.pallas_call(
        functools.partial(_sample_attn_kernel, lambda_init=lambda_init, past_len=past_len),
        grid=grid,
        in_specs=[vec(HEAD_DIM)] * 4 + [vec(V_DIM), new_spec, new_spec, new_spec, kt_spec, v_spec],
        out_specs=new_spec,
        out_shape=jax.ShapeDtypeStruct((B, tq, V_WIDTH), BF16),
        scratch_shapes=[pltpu.VMEM((n_rows, QK_WIDTH), BF16), pltpu.VMEM((n_rows, 1), F32),
                        pltpu.VMEM((n_rows, 1), F32), pltpu.VMEM((n_rows, V_DIM), F32)],
        compiler_params=pltpu.CompilerParams(dimension_semantics=("arbitrary", "arbitrary"),
                                             vmem_limit_bytes=VMEM_LIMIT),
        name="sample_attn",
    )(*lams, subln_g, qb, kb_new, vb_new, cache_kt, cache_v)


def _out_mlp_kernel(pool_ref, attn_ref, x_ref, w_out_ref, gf_ref, w_up_ref, w_down_ref, gl_ref, y_ref):
    mixed = (_dot(pool_ref[...], w_out_ref[0:POOL_WIDTH, :]) + _dot(attn_ref[...], w_out_ref[POOL_WIDTH:, :]))
    h = x_ref[...] + mixed
    ms = jnp.mean(h * h, axis=-1, keepdims=True)
    hn = (h * lax.rsqrt(ms + EPS) * gf_ref[...]).astype(BF16)
    y = h
    for c in range(D_FF // FF_CHUNK):
        cols = slice(c * FF_CHUNK, (c + 1) * FF_CHUNK)
        a = jnp.maximum(_dot(hn, w_up_ref[:, cols]), 0.0)
        y = y + _dot((a * a).astype(BF16), w_down_ref[cols, :])
    ms = jnp.mean(y * y, axis=-1, keepdims=True)
    y_ref[...] = y * lax.rsqrt(ms + EPS) * gl_ref[...]


def _out_mlp(pool, attn, x, w_out, gf, w_up, w_down, gl, *, tm):
    rows = x.shape[0]
    row_spec = lambda width: pl.BlockSpec((tm, width), lambda r: (r, 0))
    const = lambda shape: pl.BlockSpec(shape, lambda r: (0, 0), pipeline_mode=pl.Buffered(1))
    return pl.pallas_call(
        _out_mlp_kernel,
        grid=(rows // tm,),
        in_specs=[row_spec(POOL_WIDTH), row_spec(V_WIDTH), row_spec(D_MODEL),
                  const((D_MODEL, D_MODEL)), const((1, D_MODEL)), const((D_MODEL, D_FF)),
                  const((D_FF, D_MODEL)), const((1, D_MODEL))],
        out_specs=row_spec(D_MODEL),
        out_shape=jax.ShapeDtypeStruct((rows, D_MODEL), F32),
        compiler_params=pltpu.CompilerParams(dimension_semantics=("arbitrary",),
                                             vmem_limit_bytes=VMEM_LIMIT),
        name="out_mlp",
    )(pool, attn, x, w_out, gf, w_up, w_down, gl)


def _keys_from_feature_major(kt, frames):
    streams = kt.shape[0]
    return jnp.transpose(kt.reshape(streams, N_HEADS, 2, HEAD_DIM, frames), (0, 4, 1, 2, 3))[None]


def kernel(x_prompt, x_sample, state_pool, cache_k, cache_v, norm_mix_g, w_in, w_pool, pool_scale,
           lambda_q1, lambda_k1, lambda_q2, lambda_k2, subln_g, w_out, norm_ffn_g, w_up, w_down,
           norm_final_g):
    assert w_in.shape[0] == 1, "one layer per call"
    B, T, _ = x_prompt.shape
    S, TS, _ = x_sample.shape
    past_len = cache_k.shape[2]
    lambda_init = 0.8 - 0.6 * math.exp(-0.3 * 0)

    g_mix = norm_mix_g[0][None]
    g_ffn = norm_ffn_g[0][None]
    g_fin = norm_final_g[None]
    g_sub = subln_g[0][None]
    pscale = pool_scale[0][None]
    lams = (lambda_q1[0][None], lambda_k1[0][None], lambda_q2[0][None], lambda_k2[0][None])
    w_in_b = w_in[0].astype(BF16)
    w_pool_b = w_pool[0].astype(BF16)
    w_out_b = w_out[0].astype(BF16)
    w_up_b = w_up[0].astype(BF16)
    w_down_b = w_down[0].astype(BF16)

    zero_buf = jnp.zeros((B, TAIL, POOL_WIDTH), F32)
    kt_p, v_p, qb, kb, vtb, pool_p, tail_p = _proj_pool(
        x_prompt, zero_buf, g_mix, w_in_b, w_pool_b, pscale, pos0=0, bb=1, tm=PROMPT_ROWS,
        feature_major=True)
    attn_p = _prompt_attn(qb, kb, vtb, lams, g_sub.reshape(V_DIM, 1), lambda_init=lambda_init, blk=ATTN_BLOCK)
    y_p = _out_mlp(pool_p.reshape(B * T, -1), attn_p.reshape(B * T, -1), x_prompt.reshape(B * T, -1),
                   w_out_b, g_ffn, w_up_b, w_down_b, g_fin, tm=PROMPT_ROWS).reshape(B, T, D_MODEL)

    buf_s = jnp.pad(state_pool[0], ((0, 0), (TAIL - POOL_STATE, 0), (0, 0)))
    k_s, v_s, qb_s, kb_s, vb_s, pool_s, tail_s = _proj_pool(
        x_sample, buf_s, g_mix, w_in_b, w_pool_b, pscale, pos0=past_len, bb=S, tm=TS,
        feature_major=False)
    cache_kt = jnp.transpose(cache_k[0], (0, 2, 3, 4, 1)).reshape(S, QK_WIDTH, past_len)
    cache_vi = cache_v[0].reshape(S, past_len * N_HEADS, V_DIM)
    attn_s = _sample_attn(qb_s, kb_s, vb_s, cache_kt, cache_vi, lams, g_sub,
                          lambda_init=lambda_init, kblk=CACHE_BLOCK)
    y_s = _out_mlp(pool_s.reshape(S * TS, -1), attn_s.reshape(S * TS, -1), x_sample.reshape(S * TS, -1),
                   w_out_b, g_ffn, w_up_b, w_down_b, g_fin, tm=S * TS).reshape(S, TS, D_MODEL)

    return (y_p, y_s,
            tail_p[:, TAIL - POOL_STATE:][None],
            _keys_from_feature_major(kt_p, T),
            v_p.reshape(1, B, T, N_HEADS, V_DIM),
            tail_s[:, TAIL - POOL_STATE:][None],
            k_s.reshape(1, S, TS, N_HEADS, 2, HEAD_DIM),
            v_s.reshape(1, S, TS, N_HEADS, V_DIM))
```

```python
import functools
import math

import jax
import jax.numpy as jnp
from jax import lax
from jax.experimental import pallas as pl
from jax.experimental.pallas import tpu as pltpu

D_MODEL = 1024
POOL_WIDTH = 512
POOL_WINDOWS = (2, 4, 8, 16)
POOL_GROUP = 128
POOL_STATE = 15
N_HEADS = 4
HEAD_DIM = 64
V_DIM = 128
QK_WIDTH = 512
V_WIDTH = N_HEADS * V_DIM
D_FF = 4096
CHUNK_BITS = 6
HEAD_DIM_BITS = 6
assert HEAD_DIM == 1 << HEAD_DIM_BITS
EPS = 1e-6
SUBLN_EPS = 1e-5

TAIL = 16
HEAD_LANES = 2 * HEAD_DIM
SUM_ROWS = 16
VT_ROWS = V_DIM + SUM_ROWS
BIAS_TERMS = 3
BIAS_ROWS = 16
assert 3 * BIAS_TERMS <= BIAS_ROWS

BF16 = jnp.bfloat16
F32 = jnp.float32
LOG2_E = math.log2(math.e)

PROMPT_ROWS = 512
ATTN_BLOCK = 256
CACHE_BLOCK = 4096
FF_CHUNK = 1024
VMEM_LIMIT = 56 * 1024 * 1024


def _nt_dot(a, b):
    return lax.dot_general(a, b, (((1,), (1,)), ((), ())), preferred_element_type=F32)


def _dot(a, b):
    return jnp.dot(a, b, preferred_element_type=F32)


def _head_slope(h):
    return lax.bitcast_convert_type((127 - 2 * (h + 1)) << 23, F32) * LOG2_E


def _lambda(lq1, lk1, lq2, lk2, lambda_init):
    return (jnp.exp(jnp.sum(lq1[...] * lk1[...], axis=-1, keepdims=True))
            - jnp.exp(jnp.sum(lq2[...] * lk2[...], axis=-1, keepdims=True)) + lambda_init)


def _sub_ln(o0, o1, lam, g, lambda_init):
    o = o0 - lam * o1
    ms = jnp.mean(o * o, axis=-1, keepdims=True)
    return (o * lax.rsqrt(ms + SUBLN_EPS) * g) * (1.0 - lambda_init)


def _proj_pool_kernel(x_ref, g_ref, w_in_ref, w_pool_ref, pscale_ref, buf_ref,
                      k_ref, v_ref, qb_ref, kb_ref, vb_ref, pool_ref, tail_ref, ext_ref,
                      *, pos0, feature_major):
    bb, tm, _ = x_ref.shape
    rows = bb * tm
    t = pl.program_id(1)

    x = x_ref[...].reshape(rows, D_MODEL)
    ms = jnp.mean(x * x, axis=-1, keepdims=True)
    hn = (x * lax.rsqrt(ms + EPS) * g_ref[...]).astype(BF16)

    def proj(j):
        return _dot(hn, w_in_ref[:, j * 512:(j + 1) * 512])

    @pl.when(t == 0)
    def _():
        ext_ref[:, 0:TAIL, :] = buf_ref[...]

    ext_ref[:, TAIL:TAIL + tm, :] = proj(0).reshape(bb, tm, POOL_WIDTH)
    pos = pos0 + t * tm + lax.broadcasted_iota(jnp.int32, (1, tm, 1), 1)

    def pool_group(g):
        w = POOL_WINDOWS[g]
        lanes = slice(g * POOL_GROUP, (g + 1) * POOL_GROUP)
        ext = ext_ref[:, :, lanes]
        acc, span = ext, 1
        while span < w:
            acc = acc + pltpu.roll(acc, span, axis=1)
            span *= 2
        cnt = jnp.minimum(pos + 1, w).astype(F32)
        diff = (acc[:, TAIL:, :] / cnt - ext[:, TAIL:, :]).reshape(rows, POOL_GROUP).astype(BF16)
        y = _dot(diff, w_pool_ref[g]) * pscale_ref[:, lanes]
        pool_ref[:, :, lanes] = y.astype(BF16).reshape(bb, tm, POOL_GROUP)

    q = proj(1)
    qb_ref[...] = (q * (LOG2_E / math.sqrt(HEAD_DIM))).astype(BF16).reshape(bb, tm, QK_WIDTH)
    pool_group(0)
    pool_group(1)
    k = proj(2)
    kb_ref[...] = k.astype(BF16).reshape(bb, tm, QK_WIDTH)
    if feature_major:
        assert bb == 1
        k_ref[0] = k.T
    else:
        k_ref[...] = k.reshape(bb, tm, QK_WIDTH)
    pool_group(2)
    pool_group(3)
    v = proj(3)
    if feature_major:
        vt = v.T.astype(BF16)
        for h in range(N_HEADS):
            vb_ref[0, h * VT_ROWS:h * VT_ROWS + V_DIM, :] = vt[h * V_DIM:(h + 1) * V_DIM]
            vb_ref[0, h * VT_ROWS + V_DIM:(h + 1) * VT_ROWS, :] = jnp.ones((SUM_ROWS, tm), BF16)
    else:
        vb_ref[...] = v.astype(BF16).reshape(bb, tm, V_WIDTH)
    for h in range(N_HEADS):
        v_ref[:, pl.ds(h, tm, stride=N_HEADS), :] = v[:, h * V_DIM:(h + 1) * V_DIM].reshape(bb, tm, V_DIM)

    tail = ext_ref[:, tm:tm + TAIL, :]
    tail_ref[...] = tail
    ext_ref[:, 0:TAIL, :] = tail


def _proj_pool(x, buf, g, w_in, w_pool, pscale, *, pos0, bb, tm, feature_major):
    B, T, _ = x.shape
    grid = (B // bb, T // tm)
    row_spec = lambda width: pl.BlockSpec((bb, tm, width), lambda b, t: (b, t, 0))
    const = lambda shape: pl.BlockSpec(shape, lambda b, t: (0,) * len(shape), pipeline_mode=pl.Buffered(1))
    seq_spec = pl.BlockSpec((bb, TAIL, POOL_WIDTH), lambda b, t: (b, 0, 0))
    act = lambda dtype: jax.ShapeDtypeStruct((B, T, QK_WIDTH), dtype)
    if feature_major:
        fm_spec = lambda rows: pl.BlockSpec((bb, rows, tm), lambda b, t: (b, 0, t))
        fm_shape = lambda rows, dtype: jax.ShapeDtypeStruct((B, rows, T), dtype)
        k_spec, k_shape = fm_spec(QK_WIDTH), fm_shape(QK_WIDTH, F32)
        vb_spec, vb_shape = fm_spec(N_HEADS * VT_ROWS), fm_shape(N_HEADS * VT_ROWS, BF16)
    else:
        k_spec, k_shape = row_spec(QK_WIDTH), act(F32)
        vb_spec, vb_shape = row_spec(V_WIDTH), act(BF16)
    return pl.pallas_call(
        functools.partial(_proj_pool_kernel, pos0=pos0, feature_major=feature_major),
        grid=grid,
        in_specs=[row_spec(D_MODEL), const((1, D_MODEL)), const((D_MODEL, 4 * 512)),
                  const((len(POOL_WINDOWS), POOL_GROUP, POOL_GROUP)), const((1, POOL_WIDTH)), seq_spec],
        out_specs=[k_spec, pl.BlockSpec((bb, tm * N_HEADS, V_DIM), lambda b, t: (b, t, 0)),
                   row_spec(QK_WIDTH), row_spec(QK_WIDTH), vb_spec, row_spec(POOL_WIDTH), seq_spec],
        out_shape=[k_shape, jax.ShapeDtypeStruct((B, T * N_HEADS, V_DIM), F32),
                   act(BF16), act(BF16), vb_shape, act(BF16),
                   jax.ShapeDtypeStruct((B, TAIL, POOL_WIDTH), F32)],
        scratch_shapes=[pltpu.VMEM((bb, TAIL + tm, POOL_WIDTH), F32)],
        compiler_params=pltpu.CompilerParams(dimension_semantics=("arbitrary", "arbitrary"),
                                             vmem_limit_bytes=VMEM_LIMIT),
        name="proj_pool",
    )(x, g, w_in, w_pool, pscale, buf)


def _prompt_attn_kernel(lq1, lk1, lq2, lk2, gcol_ref, q_ref, k_ref, vt_ref, o_ref,
                        key_pos_ref, bias_diag_ref, s_even_ref, s_odd_ref, *, lambda_init, blk):
    n_blocks = q_ref.shape[1] // blk
    h = pl.program_id(0)
    slope = _head_slope(jnp.full((1, 1), h, jnp.int32))

    @pl.when((h == 0) & (pl.program_id(1) == 0))
    def _():
        kg = lax.broadcasted_iota(jnp.int32, key_pos_ref.shape, 0)
        lane = lax.broadcasted_iota(jnp.int32, key_pos_ref.shape, 1)
        code = jnp.where(lane < BIAS_TERMS, kg >> 8,
                         jnp.where(lane < 2 * BIAS_TERMS, kg & 255, jnp.where(lane < 3 * BIAS_TERMS, 1, 0)))
        key_pos_ref[...] = code.astype(F32).astype(BF16)

    @pl.when(pl.program_id(1) == 0)
    def _():
        kk = lax.broadcasted_iota(jnp.int32, bias_diag_ref.shape, 0)
        qq = lax.broadcasted_iota(jnp.int32, bias_diag_ref.shape, 1) & (blk - 1)
        visible = (kk >> CHUNK_BITS) <= (qq >> CHUNK_BITS)
        bias_diag_ref[...] = jnp.where(visible, -slope * jnp.abs(qq - kk).astype(F32), -jnp.inf)

    lam = _lambda(lq1, lk1, lq2, lk2, lambda_init)
    feat = lax.broadcasted_iota(jnp.int32, (HEAD_LANES, blk), 0)
    zero = jnp.zeros((HEAD_LANES, blk), BF16)
    q_off = (lax.broadcasted_iota(jnp.int32, (1, 2 * blk), 1) & (blk - 1)).astype(F32)
    bias_row = lax.broadcasted_iota(jnp.int32, (BIAS_ROWS, 2 * blk), 0)

    def pieces(x):
        out = []
        for _ in range(BIAS_TERMS):
            piece = x.astype(BF16).astype(F32)
            out.append(piece)
            x = x - piece
        return out

    s_slot = (s_even_ref, s_odd_ref)
    run_time_zero = jnp.minimum(pl.program_id(1), 0)

    def sublane_max(s):
        return jnp.max(s.reshape(blk // 8, 8, s.shape[1]), axis=0)

    def scores(c, col_max):
        n = c * blk
        qt = q_ref[0, n:n + blk, :].astype(F32).T.astype(BF16)
        q_cols = jnp.concatenate([jnp.where(feat < HEAD_DIM, qt, zero), jnp.where(feat >= HEAD_DIM, qt, zero)],
                                 axis=1)
        s_diag = _dot(k_ref[0, n:n + blk, :], q_cols) + bias_diag_ref[...]
        s_slot[c % 2][n:n + blk, :] = s_diag
        m = sublane_max(s_diag)
        if c > 0:
            terms = pieces(slope * 256.0) + pieces(slope) + pieces(-slope * (q_off + float(n)))
            bias_cols = jnp.zeros(bias_row.shape, F32)
            for r, term in enumerate(terms):
                bias_cols = jnp.where(bias_row == r, term, bias_cols)
            w = jnp.concatenate([q_cols, bias_cols.astype(BF16),
                                 jnp.zeros((HEAD_LANES - BIAS_ROWS, 2 * blk), BF16)], axis=0)
            for j in range(c):
                rows = slice(j * blk, (j + 1) * blk)
                s_past = _dot(jnp.concatenate([k_ref[0, rows, :], key_pos_ref[rows, :]], axis=1), w)
                s_slot[c % 2][rows, :] = s_past
                m = jnp.maximum(m, sublane_max(s_past))
        col_max[c] = jnp.max(m, axis=0, keepdims=True)

    def outputs(c, m):
        n = c * blk
        acc = None
        for j in range(c + 1):
            rows = slice(j * blk, (j + 1) * blk)
            s = s_slot[c % 2][pl.ds(pl.multiple_of(run_time_zero + j * blk, blk), blk), :]
            pv = _dot(vt_ref[0, :, rows], jnp.exp2(s - m).astype(BF16))
            acc = pv if acc is None else acc + pv
        o = [acc[0:V_DIM, half * blk:(half + 1) * blk] / acc[V_DIM:V_DIM + 1, half * blk:(half + 1) * blk]
             for half in range(2)]
        ot = o[0] - lam * o[1]
        ms = jnp.mean(ot * ot, axis=0, keepdims=True)
        yt = (ot * lax.rsqrt(ms + SUBLN_EPS) * gcol_ref[...]) * (1.0 - lambda_init)
        o_ref[0, n:n + blk, :] = yt.T.astype(o_ref.dtype)

    col_max = {}
    scores(n_blocks - 1, col_max)
    for c in reversed(range(n_blocks)):
        if c > 0:
            scores(c - 1, col_max)
        outputs(c, col_max[c])


def _prompt_attn(qb, kb, vtb, lams, subln_gcol, *, lambda_init, blk):
    B, T, _ = qb.shape
    grid = (N_HEADS, B)
    vec = lambda n: pl.BlockSpec((1, n), lambda h, b: (0, 0))
    qk_spec = pl.BlockSpec((1, T, HEAD_LANES), lambda h, b: (b, 0, h))
    vt_spec = pl.BlockSpec((1, VT_ROWS, T), lambda h, b: (b, h, 0))
    return pl.pallas_call(
        functools.partial(_prompt_attn_kernel, lambda_init=lambda_init, blk=blk),
        grid=grid,
        in_specs=[vec(HEAD_DIM)] * 4 + [pl.BlockSpec((V_DIM, 1), lambda h, b: (0, 0)), qk_spec, qk_spec, vt_spec],
        out_specs=qk_spec,
        out_shape=jax.ShapeDtypeStruct((B, T, V_WIDTH), BF16),
        scratch_shapes=[pltpu.VMEM((T - blk, HEAD_LANES), BF16), pltpu.VMEM((blk, 2 * blk), F32),
                        pltpu.VMEM((T, 2 * blk), F32), pltpu.VMEM((T, 2 * blk), F32)],
        compiler_params=pltpu.CompilerParams(dimension_semantics=("arbitrary",) * 2,
                                             vmem_limit_bytes=VMEM_LIMIT),
        name="prompt_attn",
    )(*lams, subln_gcol, qb, kb, vtb)


def _sample_attn_kernel(lq1, lk1, lq2, lk2, g_ref, q_ref, kn_ref, vn_ref, ckt_ref, cv_ref, o_ref,
                        qrows_ref, m_ref, l_ref, acc_ref, *, lambda_init, past_len):
    tq = q_ref.shape[1]
    head_rows = 2 * tq
    n_rows = N_HEADS * head_rows
    kb = ckt_ref.shape[2]
    j = pl.program_id(1)
    nj = pl.num_programs(1)

    tq_bits = tq.bit_length() - 1
    assert tq == 1 << tq_bits
    row = lax.broadcasted_iota(jnp.int32, (n_rows, 1), 0)
    q_pos = past_len + (row & (tq - 1))
    slope = _head_slope(row >> (tq_bits + 1))

    @pl.when(j == 0)
    def _():
        q = q_ref[0]
        tiled = jnp.concatenate([q] * (2 * N_HEADS), axis=0)
        r = lax.broadcasted_iota(jnp.int32, tiled.shape, 0)
        c = lax.broadcasted_iota(jnp.int32, tiled.shape, 1)
        qrows_ref[...] = jnp.where((r >> tq_bits) == (c >> HEAD_DIM_BITS), tiled, jnp.zeros_like(tiled))
        m_ref[...] = jnp.full(m_ref.shape, -jnp.inf, F32)
        l_ref[...] = jnp.zeros(l_ref.shape, F32)
        acc_ref[...] = jnp.zeros(acc_ref.shape, F32)

    def update(s, k_pos, head_values):
        s = s - slope * jnp.abs(q_pos - k_pos).astype(F32)
        m_old = m_ref[...]
        m_new = jnp.maximum(m_old, jnp.max(s, axis=-1, keepdims=True))
        p = jnp.exp2(s - m_new)
        alpha = jnp.exp2(m_old - m_new)
        l_ref[...] = alpha * l_ref[...] + jnp.sum(p, axis=-1, keepdims=True)
        m_ref[...] = m_new
        p = p.astype(BF16)
        for h in range(N_HEADS):
            rows = slice(h * head_rows, (h + 1) * head_rows)
            acc_ref[rows, :] = alpha[rows] * acc_ref[rows, :] + _dot(p[rows], head_values(h))

    update(_dot(qrows_ref[...], ckt_ref[0].astype(BF16)),
           j * kb + lax.broadcasted_iota(jnp.int32, (1, kb), 1),
           lambda h: cv_ref[0, pl.ds(h, kb, stride=N_HEADS), :].astype(BF16))

    @pl.when(j == nj - 1)
    def _():
        update(_nt_dot(qrows_ref[...], kn_ref[0]),
               past_len + lax.broadcasted_iota(jnp.int32, (1, tq), 1),
               lambda h: vn_ref[0, :, h * V_DIM:(h + 1) * V_DIM])
        lam = _lambda(lq1, lk1, lq2, lk2, lambda_init)
        out = acc_ref[...] / l_ref[...]
        for h in range(N_HEADS):
            o0 = out[h * head_rows:h * head_rows + tq]
            o1 = out[h * head_rows + tq:(h + 1) * head_rows]
            o_ref[0, :, h * V_DIM:(h + 1) * V_DIM] = _sub_ln(o0, o1, lam, g_ref[...], lambda_init).astype(o_ref.dtype)


def _sample_attn(qb, kb_new, vb_new, cache_kt, cache_v, lams, subln_g, *, lambda_init, kblk):
    B, tq, _ = qb.shape
    past_len = cache_kt.shape[2]
    n_rows = 2 * N_HEADS * tq
    grid = (B, past_len // kblk)
    vec = lambda n: pl.BlockSpec((1, n), lambda b, j: (0, 0))
    new_spec = pl.BlockSpec((1, tq, QK_WIDTH), lambda b, j: (b, 0, 0))
    kt_spec = pl.BlockSpec((1, QK_WIDTH, kblk), lambda b, j: (b, 0, j))
    v_spec = pl.BlockSpec((1, kblk * N_HEADS, V_DIM), lambda b, j: (b, j, 0))
    return pl.pallas_call(
        functools.partial(_sample_attn_kernel, lambda_init=lambda_init, past_len=past_len),
        grid=grid,
        in_specs=[vec(HEAD_DIM)] * 4 + [vec(V_DIM), new_spec, new_spec, new_spec, kt_spec, v_spec],
        out_specs=new_spec,
        out_shape=jax.ShapeDtypeStruct((B, tq, V_WIDTH), BF16),
        scratch_shapes=[pltpu.VMEM((n_rows, QK_WIDTH), BF16), pltpu.VMEM((n_rows, 1), F32),
                        pltpu.VMEM((n_rows, 1), F32), pltpu.VMEM((n_rows, V_DIM), F32)],
        compiler_params=pltpu.CompilerParams(dimension_semantics=("arbitrary", "arbitrary"),
                                             vmem_limit_bytes=VMEM_LIMIT),
        name="sample_attn",
    )(*lams, subln_g, qb, kb_new, vb_new, cache_kt, cache_v)


def _out_mlp_kernel(pool_ref, attn_ref, x_ref, w_out_ref, gf_ref, w_up_ref, w_down_ref, gl_ref, y_ref):
    mixed = (_dot(pool_ref[...], w_out_ref[0:POOL_WIDTH, :]) + _dot(attn_ref[...], w_out_ref[POOL_WIDTH:, :]))
    h = x_ref[...] + mixed
    ms = jnp.mean(h * h, axis=-1, keepdims=True)
    hn = (h * lax.rsqrt(ms + EPS) * gf_ref[...]).astype(BF16)
    y = h
    for c in range(D_FF // FF_CHUNK):
        cols = slice(c * FF_CHUNK, (c + 1) * FF_CHUNK)
        a = jnp.maximum(_dot(hn, w_up_ref[:, cols]), 0.0)
        y = y + _dot((a * a).astype(BF16), w_down_ref[cols, :])
    ms = jnp.mean(y * y, axis=-1, keepdims=True)
    y_ref[...] = y * lax.rsqrt(ms + EPS) * gl_ref[...]


def _out_mlp(pool, attn, x, w_out, gf, w_up, w_down, gl, *, tm):
    rows = x.shape[0]
    row_spec = lambda width: pl.BlockSpec((tm, width), lambda r: (r, 0))
    const = lambda shape: pl.BlockSpec(shape, lambda r: (0, 0), pipeline_mode=pl.Buffered(1))
    return pl.pallas_call(
        _out_mlp_kernel,
        grid=(rows // tm,),
        in_specs=[row_spec(POOL_WIDTH), row_spec(V_WIDTH), row_spec(D_MODEL),
                  const((D_MODEL, D_MODEL)), const((1, D_MODEL)), const((D_MODEL, D_FF)),
                  const((D_FF, D_MODEL)), const((1, D_MODEL))],
        out_specs=row_spec(D_MODEL),
        out_shape=jax.ShapeDtypeStruct((rows, D_MODEL), F32),
        compiler_params=pltpu.CompilerParams(dimension_semantics=("arbitrary",),
                                             vmem_limit_bytes=VMEM_LIMIT),
        name="out_mlp",
    )(pool, attn, x, w_out, gf, w_up, w_down, gl)


def _keys_from_feature_major(kt, frames):
    streams = kt.shape[0]
    return jnp.transpose(kt.reshape(streams, N_HEADS, 2, HEAD_DIM, frames), (0, 4, 1, 2, 3))[None]


def kernel(x_prompt, x_sample, state_pool, cache_k, cache_v, norm_mix_g, w_in, w_pool, pool_scale,
           lambda_q1, lambda_k1, lambda_q2, lambda_k2, subln_g, w_out, norm_ffn_g, w_up, w_down,
           norm_final_g):
    assert w_in.shape[0] == 1, "one layer per call"
    B, T, _ = x_prompt.shape
    S, TS, _ = x_sample.shape
    past_len = cache_k.shape[2]
    lambda_init = 0.8 - 0.6 * math.exp(-0.3 * 0)

    g_mix = norm_mix_g[0][None]
    g_ffn = norm_ffn_g[0][None]
    g_fin = norm_final_g[None]
    g_sub = subln_g[0][None]
    pscale = pool_scale[0][None]
    lams = (lambda_q1[0][None], lambda_k1[0][None], lambda_q2[0][None], lambda_k2[0][None])
    w_in_b = w_in[0].astype(BF16)
    w_pool_b = w_pool[0].astype(BF16)
    w_out_b = w_out[0].astype(BF16)
    w_up_b = w_up[0].astype(BF16)
    w_down_b = w_down[0].astype(BF16)

    zero_buf = jnp.zeros((B, TAIL, POOL_WIDTH), F32)
    kt_p, v_p, qb, kb, vtb, pool_p, tail_p = _proj_pool(
        x_prompt, zero_buf, g_mix, w_in_b, w_pool_b, pscale, pos0=0, bb=1, tm=PROMPT_ROWS,
        feature_major=True)
    attn_p = _prompt_attn(qb, kb, vtb, lams, g_sub.reshape(V_DIM, 1), lambda_init=lambda_init, blk=ATTN_BLOCK)
    y_p = _out_mlp(pool_p.reshape(B * T, -1), attn_p.reshape(B * T, -1), x_prompt.reshape(B * T, -1),
                   w_out_b, g_ffn, w_up_b, w_down_b, g_fin, tm=PROMPT_ROWS).reshape(B, T, D_MODEL)

    buf_s = jnp.pad(state_pool[0], ((0, 0), (TAIL - POOL_STATE, 0), (0, 0)))
    k_s, v_s, qb_s, kb_s, vb_s, pool_s, tail_s = _proj_pool(
        x_sample, buf_s, g_mix, w_in_b, w_pool_b, pscale, pos0=past_len, bb=S, tm=TS,
        feature_major=False)
    cache_kt = jnp.transpose(cache_k[0], (0, 2, 3, 4, 1)).reshape(S, QK_WIDTH, past_len)
    cache_vi = cache_v[0].reshape(S, past_len * N_HEADS, V_DIM)
    attn_s = _sample_attn(qb_s, kb_s, vb_s, cache_kt, cache_vi, lams, g_sub,
                          lambda_init=lambda_init, kblk=CACHE_BLOCK)
    y_s = _out_mlp(pool_s.reshape(S * TS, -1), attn_s.reshape(S * TS, -1), x_sample.reshape(S * TS, -1),
                   w_out_b, g_ffn, w_up_b, w_down_b, g_fin, tm=S * TS).reshape(S, TS, D_MODEL)

    return (y_p, y_s,
            tail_p[:, TAIL - POOL_STATE:][None],
            _keys_from_feature_major(kt_p, T),
            v_p.reshape(1, B, T, N_HEADS, V_DIM),
            tail_s[:, TAIL - POOL_STATE:][None],
            k_s.reshape(1, S, TS, N_HEADS, 2, HEAD_DIM),
            v_s.reshape(1, S, TS, N_HEADS, V_DIM))
```

```python
import functools
import math

import jax
import jax.numpy as jnp
from jax import lax
from jax.experimental import pallas as pl
from jax.experimental.pallas import tpu as pltpu

D_MODEL = 1024
POOL_WIDTH = 512
POOL_WINDOWS = (2, 4, 8, 16)
POOL_GROUP = 128
POOL_STATE = 15
N_HEADS = 4
HEAD_DIM = 64
V_DIM = 128
QK_WIDTH = 512
V_WIDTH = N_HEADS * V_DIM
D_FF = 4096
CHUNK_BITS = 6
HEAD_DIM_BITS = 6
assert HEAD_DIM == 1 << HEAD_DIM_BITS
EPS = 1e-6
SUBLN_EPS = 1e-5

TAIL = 16
HEAD_LANES = 2 * HEAD_DIM
SUM_ROWS = 16
VT_ROWS = V_DIM + SUM_ROWS
BIAS_TERMS = 3
BIAS_ROWS = 16
assert 3 * BIAS_TERMS <= BIAS_ROWS

BF16 = jnp.bfloat16
F32 = jnp.float32
LOG2_E = math.log2(math.e)

PROMPT_ROWS = 512
ATTN_BLOCK = 256
FF_CHUNK = 1024
VMEM_LIMIT = 56 * 1024 * 1024


def _nt_dot(a, b):
    return lax.dot_general(a, b, (((1,), (1,)), ((), ())), preferred_element_type=F32)


def _dot(a, b):
    return jnp.dot(a, b, preferred_element_type=F32)


def _head_slope(h):
    return lax.bitcast_convert_type((127 - 2 * (h + 1)) << 23, F32) * LOG2_E


def _lambda(lq1, lk1, lq2, lk2, lambda_init):
    return (jnp.exp(jnp.sum(lq1[...] * lk1[...], axis=-1, keepdims=True))
            - jnp.exp(jnp.sum(lq2[...] * lk2[...], axis=-1, keepdims=True)) + lambda_init)


def _sub_ln(o0, o1, lam, g, lambda_init):
    o = o0 - lam * o1
    ms = jnp.mean(o * o, axis=-1, keepdims=True)
    return (o * lax.rsqrt(ms + SUBLN_EPS) * g) * (1.0 - lambda_init)


def _proj_pool_kernel(x_ref, g_ref, w_in_ref, w_pool_ref, pscale_ref, buf_ref,
                      k_ref, v_ref, qb_ref, kb_ref, vb_ref, pool_ref, tail_ref, ext_ref,
                      *, pos0, feature_major):
    bb, tm, _ = x_ref.shape
    rows = bb * tm
    t = pl.program_id(1)

    x = x_ref[...].reshape(rows, D_MODEL)
    ms = jnp.mean(x * x, axis=-1, keepdims=True)
    hn = (x * lax.rsqrt(ms + EPS) * g_ref[...]).astype(BF16)

    def proj(j):
        return _dot(hn, w_in_ref[:, j * 512:(j + 1) * 512])

    @pl.when(t == 0)
    def _():
        ext_ref[:, 0:TAIL, :] = buf_ref[...]

    ext_ref[:, TAIL:TAIL + tm, :] = proj(0).reshape(bb, tm, POOL_WIDTH)
    pos = pos0 + t * tm + lax.broadcasted_iota(jnp.int32, (1, tm, 1), 1)

    def pool_group(g):
        w = POOL_WINDOWS[g]
        lanes = slice(g * POOL_GROUP, (g + 1) * POOL_GROUP)
        ext = ext_ref[:, :, lanes]
        acc, span = ext, 1
        while span < w:
            acc = acc + pltpu.roll(acc, span, axis=1)
            span *= 2
        cnt = jnp.minimum(pos + 1, w).astype(F32)
        diff = (acc[:, TAIL:, :] / cnt - ext[:, TAIL:, :]).reshape(rows, POOL_GROUP).astype(BF16)
        y = _dot(diff, w_pool_ref[g]) * pscale_ref[:, lanes]
        pool_ref[:, :, lanes] = y.astype(BF16).reshape(bb, tm, POOL_GROUP)

    q = proj(1)
    qb_ref[...] = (q * (LOG2_E / math.sqrt(HEAD_DIM))).astype(BF16).reshape(bb, tm, QK_WIDTH)
    pool_group(0)
    pool_group(1)
    k = proj(2)
    kb_ref[...] = k.astype(BF16).reshape(bb, tm, QK_WIDTH)
    if feature_major:
        assert bb == 1
        k_ref[0] = k.T
    else:
        k_ref[...] = k.reshape(bb, tm, QK_WIDTH)
    pool_group(2)
    pool_group(3)
    v = proj(3)
    if feature_major:
        vt = v.T.astype(BF16)
        for h in range(N_HEADS):
            vb_ref[0, h * VT_ROWS:h * VT_ROWS + V_DIM, :] = vt[h * V_DIM:(h + 1) * V_DIM]
            vb_ref[0, h * VT_ROWS + V_DIM:(h + 1) * VT_ROWS, :] = jnp.ones((SUM_ROWS, tm), BF16)
    else:
        vb_ref[...] = v.astype(BF16).reshape(bb, tm, V_WIDTH)
    for h in range(N_HEADS):
        v_ref[:, pl.ds(h, tm, stride=N_HEADS), :] = v[:, h * V_DIM:(h + 1) * V_DIM].reshape(bb, tm, V_DIM)

    tail = ext_ref[:, tm:tm + TAIL, :]
    tail_ref[...] = tail
    ext_ref[:, 0:TAIL, :] = tail


def _proj_pool(x, buf, g, w_in, w_pool, pscale, *, pos0, bb, tm, feature_major):
    B, T, _ = x.shape
    grid = (B // bb, T // tm)
    row_spec = lambda width: pl.BlockSpec((bb, tm, width), lambda b, t: (b, t, 0))
    const = lambda shape: pl.BlockSpec(shape, lambda b, t: (0,) * len(shape), pipeline_mode=pl.Buffered(1))
    seq_spec = pl.BlockSpec((bb, TAIL, POOL_WIDTH), lambda b, t: (b, 0, 0))
    act = lambda dtype: jax.ShapeDtypeStruct((B, T, QK_WIDTH), dtype)
    if feature_major:
        fm_spec = lambda rows: pl.BlockSpec((bb, rows, tm), lambda b, t: (b, 0, t))
        fm_shape = lambda rows, dtype: jax.ShapeDtypeStruct((B, rows, T), dtype)
        k_spec, k_shape = fm_spec(QK_WIDTH), fm_shape(QK_WIDTH, F32)
        vb_spec, vb_shape = fm_spec(N_HEADS * VT_ROWS), fm_shape(N_HEADS * VT_ROWS, BF16)
    else:
        k_spec, k_shape = row_spec(QK_WIDTH), act(F32)
        vb_spec, vb_shape = row_spec(V_WIDTH), act(BF16)
    return pl.pallas_call(
        functools.partial(_proj_pool_kernel, pos0=pos0, feature_major=feature_major),
        grid=grid,
        in_specs=[row_spec(D_MODEL), const((1, D_MODEL)), const((D_MODEL, 4 * 512)),
                  const((len(POOL_WINDOWS), POOL_GROUP, POOL_GROUP)), const((1, POOL_WIDTH)), seq_spec],
        out_specs=[k_spec, pl.BlockSpec((bb, tm * N_HEADS, V_DIM), lambda b, t: (b, t, 0)),
                   row_spec(QK_WIDTH), row_spec(QK_WIDTH), vb_spec, row_spec(POOL_WIDTH), seq_spec],
        out_shape=[k_shape, jax.ShapeDtypeStruct((B, T * N_HEADS, V_DIM), F32),
                   act(BF16), act(BF16), vb_shape, act(BF16),
                   jax.ShapeDtypeStruct((B, TAIL, POOL_WIDTH), F32)],
        scratch_shapes=[pltpu.VMEM((bb, TAIL + tm, POOL_WIDTH), F32)],
        compiler_params=pltpu.CompilerParams(dimension_semantics=("arbitrary", "arbitrary"),
                                             vmem_limit_bytes=VMEM_LIMIT),
        name="proj_pool",
    )(x, g, w_in, w_pool, pscale, buf)


def _prompt_attn_kernel(lq1, lk1, lq2, lk2, gcol_ref, q_ref, k_ref, vt_ref, o_ref,
                        key_pos_ref, bias_diag_ref, s_even_ref, s_odd_ref, *, lambda_init, blk):
    n_blocks = q_ref.shape[1] // blk
    h = pl.program_id(0)
    slope = _head_slope(jnp.full((1, 1), h, jnp.int32))

    @pl.when((h == 0) & (pl.program_id(1) == 0))
    def _():
        kg = lax.broadcasted_iota(jnp.int32, key_pos_ref.shape, 0)
        lane = lax.broadcasted_iota(jnp.int32, key_pos_ref.shape, 1)
        code = jnp.where(lane < BIAS_TERMS, kg >> 8,
                         jnp.where(lane < 2 * BIAS_TERMS, kg & 255, jnp.where(lane < 3 * BIAS_TERMS, 1, 0)))
        key_pos_ref[...] = code.astype(F32).astype(BF16)

    @pl.when(pl.program_id(1) == 0)
    def _():
        kk = lax.broadcasted_iota(jnp.int32, bias_diag_ref.shape, 0)
        qq = lax.broadcasted_iota(jnp.int32, bias_diag_ref.shape, 1) & (blk - 1)
        visible = (kk >> CHUNK_BITS) <= (qq >> CHUNK_BITS)
        bias_diag_ref[...] = jnp.where(visible, -slope * jnp.abs(qq - kk).astype(F32), -jnp.inf)

    lam = _lambda(lq1, lk1, lq2, lk2, lambda_init)
    feat = lax.broadcasted_iota(jnp.int32, (HEAD_LANES, blk), 0)
    zero = jnp.zeros((HEAD_LANES, blk), BF16)
    q_off = (lax.broadcasted_iota(jnp.int32, (1, 2 * blk), 1) & (blk - 1)).astype(F32)
    bias_row = lax.broadcasted_iota(jnp.int32, (BIAS_ROWS, 2 * blk), 0)

    def pieces(x):
        out = []
        for _ in range(BIAS_TERMS):
            piece = x.astype(BF16).astype(F32)
            out.append(piece)
            x = x - piece
        return out

    s_slot = (s_even_ref, s_odd_ref)
    run_time_zero = jnp.minimum(pl.program_id(1), 0)

    def sublane_max(s):
        return jnp.max(s.reshape(blk // 8, 8, s.shape[1]), axis=0)

    def scores(c, col_max):
        n = c * blk
        qt = q_ref[0, n:n + blk, :].astype(F32).T.astype(BF16)
        q_cols = jnp.concatenate([jnp.where(feat < HEAD_DIM, qt, zero), jnp.where(feat >= HEAD_DIM, qt, zero)],
                                 axis=1)
        s_diag = _dot(k_ref[0, n:n + blk, :], q_cols) + bias_diag_ref[...]
        s_slot[c % 2][n:n + blk, :] = s_diag
        m = sublane_max(s_diag)
        if c > 0:
            terms = pieces(slope * 256.0) + pieces(slope) + pieces(-slope * (q_off + float(n)))
            bias_cols = jnp.zeros(bias_row.shape, F32)
            for r, term in enumerate(terms):
                bias_cols = jnp.where(bias_row == r, term, bias_cols)
            w = jnp.concatenate([q_cols, bias_cols.astype(BF16),
                                 jnp.zeros((HEAD_LANES - BIAS_ROWS, 2 * blk), BF16)], axis=0)
            for j in range(c):
                rows = slice(j * blk, (j + 1) * blk)
                s_past = _dot(jnp.concatenate([k_ref[0, rows, :], key_pos_ref[rows, :]], axis=1), w)
                s_slot[c % 2][rows, :] = s_past
                m = jnp.maximum(m, sublane_max(s_past))
        col_max[c] = jnp.max(m, axis=0, keepdims=True)

    def outputs(c, m):
        n = c * blk
        acc = None
        for j in range(c + 1):
            rows = slice(j * blk, (j + 1) * blk)
            s = s_slot[c % 2][pl.ds(pl.multiple_of(run_time_zero + j * blk, blk), blk), :]
            pv = _dot(vt_ref[0, :, rows], jnp.exp2(s - m).astype(BF16))
            acc = pv if acc is None else acc + pv
        o = [acc[0:V_DIM, half * blk:(half + 1) * blk] / acc[V_DIM:V_DIM + 1, half * blk:(half + 1) * blk]
             for half in range(2)]
        ot = o[0] - lam * o[1]
        ms = jnp.mean(ot * ot, axis=0, keepdims=True)
        yt = (ot * lax.rsqrt(ms + SUBLN_EPS) * gcol_ref[...]) * (1.0 - lambda_init)
        o_ref[0, n:n + blk, :] = yt.T.astype(o_ref.dtype)

    col_max = {}
    scores(n_blocks - 1, col_max)
    for c in reversed(range(n_blocks)):
        if c > 0:
            scores(c - 1, col_max)
        outputs(c, col_max[c])


def _prompt_attn(qb, kb, vtb, lams, subln_gcol, *, lambda_init, blk):
    B, T, _ = qb.shape
    grid = (N_HEADS, B)
    vec = lambda n: pl.BlockSpec((1, n), lambda h, b: (0, 0))
    qk_spec = pl.BlockSpec((1, T, HEAD_LANES), lambda h, b: (b, 0, h))
    vt_spec = pl.BlockSpec((1, VT_ROWS, T), lambda h, b: (b, h, 0))
    return pl.pallas_call(
        functools.partial(_prompt_attn_kernel, lambda_init=lambda_init, blk=blk),
        grid=grid,
        in_specs=[vec(HEAD_DIM)] * 4 + [pl.BlockSpec((V_DIM, 1), lambda h, b: (0, 0)), qk_spec, qk_spec, vt_spec],
        out_specs=qk_spec,
        out_shape=jax.ShapeDtypeStruct((B, T, V_WIDTH), BF16),
        scratch_shapes=[pltpu.VMEM((T - blk, HEAD_LANES), BF16), pltpu.VMEM((blk, 2 * blk), F32),
                        pltpu.VMEM((T, 2 * blk), F32), pltpu.VMEM((T, 2 * blk), F32)],
        compiler_params=pltpu.CompilerParams(dimension_semantics=("arbitrary",) * 2,
                                             vmem_limit_bytes=VMEM_LIMIT),
        name="prompt_attn",
    )(*lams, subln_gcol, qb, kb, vtb)


def _sample_attn_step(j, nj, lq1, lk1, lq2, lk2, g_ref, q_ref, kn_ref, vn_ref, ckt_ref, cv_ref, o_ref,
                      qrows_ref, m_ref, l_ref, acc_ref, *, lambda_init, past_len):
    tq = q_ref.shape[1]
    head_rows = 2 * tq
    n_rows = N_HEADS * head_rows
    kb = ckt_ref.shape[2]

    tq_bits = tq.bit_length() - 1
    assert tq == 1 << tq_bits
    row = lax.broadcasted_iota(jnp.int32, (n_rows, 1), 0)
    q_rel = row & (tq - 1)
    slope = _head_slope(row >> (tq_bits + 1))

    @pl.when(j == 0)
    def _():
        q = q_ref[0]
        tiled = jnp.concatenate([q] * (2 * N_HEADS), axis=0)
        r = lax.broadcasted_iota(jnp.int32, tiled.shape, 0)
        c = lax.broadcasted_iota(jnp.int32, tiled.shape, 1)
        qrows_ref[...] = jnp.where((r >> tq_bits) == (c >> HEAD_DIM_BITS), tiled, jnp.zeros_like(tiled))
        m_ref[...] = jnp.full(m_ref.shape, -jnp.inf, F32)
        l_ref[...] = jnp.zeros(l_ref.shape, F32)
        acc_ref[...] = jnp.zeros(acc_ref.shape, F32)

    def update(s, values):
        m_old = m_ref[...]
        m_new = jnp.maximum(m_old, jnp.max(s, axis=-1, keepdims=True))
        p = jnp.exp2(s - m_new)
        alpha = jnp.exp2(m_old - m_new)
        l_ref[...] = alpha * l_ref[...] + jnp.sum(p, axis=-1, keepdims=True)
        m_ref[...] = m_new
        acc_ref[...] = alpha * acc_ref[...] + _dot(p.astype(BF16), values)

    k_rel = (j * kb - past_len + lax.broadcasted_iota(jnp.int32, (1, kb), 1)).astype(F32)
    values = jnp.concatenate([cv_ref[0, pl.ds(h, kb, stride=N_HEADS), :].astype(BF16) for h in range(N_HEADS)],
                             axis=1)
    update(_dot(qrows_ref[...], ckt_ref[0].astype(BF16)) + slope * k_rel, values)

    @pl.when(j == nj - 1)
    def _():
        k_new = lax.broadcasted_iota(jnp.int32, (1, tq), 1)
        update(_nt_dot(qrows_ref[...], kn_ref[0]) + slope * (q_rel - jnp.abs(q_rel - k_new)).astype(F32),
               vn_ref[0])
        lam = _lambda(lq1, lk1, lq2, lk2, lambda_init)
        out = acc_ref[...] / l_ref[...]
        for h in range(N_HEADS):
            own = out[h * head_rows:(h + 1) * head_rows, h * V_DIM:(h + 1) * V_DIM]
            o_ref[0, :, h * V_DIM:(h + 1) * V_DIM] = _sub_ln(own[:tq], own[tq:], lam, g_ref[...],
                                                             lambda_init).astype(o_ref.dtype)


def _out_mlp_kernel(pool_ref, attn_ref, x_ref, w_out_ref, gf_ref, w_up_ref, w_down_ref, gl_ref, y_ref):
    _out_mlp_tile(pool_ref, attn_ref, x_ref, w_out_ref, gf_ref, w_up_ref, w_down_ref, gl_ref, y_ref)


def _out_mlp_tile(pool_ref, attn_ref, x_ref, w_out_ref, gf_ref, w_up_ref, w_down_ref, gl_ref, y_ref):
    mixed = (_dot(pool_ref[...], w_out_ref[0:POOL_WIDTH, :]) + _dot(attn_ref[...], w_out_ref[POOL_WIDTH:, :]))
    h = x_ref[...] + mixed
    ms = jnp.mean(h * h, axis=-1, keepdims=True)
    hn = (h * lax.rsqrt(ms + EPS) * gf_ref[...]).astype(BF16)
    y = h
    for c in range(D_FF // FF_CHUNK):
        cols = slice(c * FF_CHUNK, (c + 1) * FF_CHUNK)
        a = jnp.maximum(_dot(hn, w_up_ref[:, cols]), 0.0)
        y = y + _dot((a * a).astype(BF16), w_down_ref[cols, :])
    ms = jnp.mean(y * y, axis=-1, keepdims=True)
    y_ref[...] = y * lax.rsqrt(ms + EPS) * gl_ref[...]


def _out_mlp(pool, attn, x, w_out, gf, w_up, w_down, gl, *, tm):
    rows = x.shape[0]
    row_spec = lambda width: pl.BlockSpec((tm, width), lambda r: (r, 0))
    const = lambda shape: pl.BlockSpec(shape, lambda r: (0, 0), pipeline_mode=pl.Buffered(1))
    return pl.pallas_call(
        _out_mlp_kernel,
        grid=(rows // tm,),
        in_specs=[row_spec(POOL_WIDTH), row_spec(V_WIDTH), row_spec(D_MODEL),
                  const((D_MODEL, D_MODEL)), const((1, D_MODEL)), const((D_MODEL, D_FF)),
                  const((D_FF, D_MODEL)), const((1, D_MODEL))],
        out_specs=row_spec(D_MODEL),
        out_shape=jax.ShapeDtypeStruct((rows, D_MODEL), F32),
        compiler_params=pltpu.CompilerParams(dimension_semantics=("arbitrary",),
                                             vmem_limit_bytes=VMEM_LIMIT),
        name="out_mlp",
    )(pool, attn, x, w_out, gf, w_up, w_down, gl)


def _out_mlp_sample_attn_kernel(*refs, parts, lambda_init, past_len):
    mlp_in, samp_in, (y_ref, o_ref), scratch = refs[:8], refs[8:18], refs[18:20], refs[20:]
    r = pl.program_id(0)
    _sample_attn_step(r % parts, parts, *samp_in, o_ref, *scratch, lambda_init=lambda_init, past_len=past_len)
    _out_mlp_tile(*mlp_in, y_ref)


def _out_mlp_with_sample_attn(pool, attn, x, w_out, gf, w_up, w_down, gl,
                              qb, kb_new, vb_new, cache_kt, cache_v, lams, subln_g, *, tm, lambda_init):
    rows = x.shape[0]
    S, tq, _ = qb.shape
    past_len = cache_kt.shape[2]
    steps = rows // tm
    assert steps % S == 0, "every stream gets the same number of grid steps"
    parts = steps // S
    kblk = past_len // parts
    n_rows = 2 * N_HEADS * tq
    row_spec = lambda width: pl.BlockSpec((tm, width), lambda r: (r, 0))
    const = lambda shape: pl.BlockSpec(shape, lambda r: (0, 0), pipeline_mode=pl.Buffered(1))
    vec = lambda n: pl.BlockSpec((1, n), lambda r: (0, 0))
    new_spec = pl.BlockSpec((1, tq, QK_WIDTH), lambda r: (r // parts, 0, 0))
    kt_spec = pl.BlockSpec((1, QK_WIDTH, kblk), lambda r: (r // parts, 0, r % parts))
    v_spec = pl.BlockSpec((1, kblk * N_HEADS, V_DIM), lambda r: (r // parts, r % parts, 0))
    return pl.pallas_call(
        functools.partial(_out_mlp_sample_attn_kernel, parts=parts, lambda_init=lambda_init, past_len=past_len),
        grid=(steps,),
        in_specs=[row_spec(POOL_WIDTH), row_spec(V_WIDTH), row_spec(D_MODEL),
                  const((D_MODEL, D_MODEL)), const((1, D_MODEL)), const((D_MODEL, D_FF)),
                  const((D_FF, D_MODEL)), const((1, D_MODEL))]
                 + [vec(HEAD_DIM)] * 4 + [vec(V_DIM), new_spec, new_spec, new_spec, kt_spec, v_spec],
        out_specs=[row_spec(D_MODEL), new_spec],
        out_shape=[jax.ShapeDtypeStruct((rows, D_MODEL), F32), jax.ShapeDtypeStruct((S, tq, V_WIDTH), BF16)],
        scratch_shapes=[pltpu.VMEM((n_rows, QK_WIDTH), BF16), pltpu.VMEM((n_rows, 1), F32),
                        pltpu.VMEM((n_rows, 1), F32), pltpu.VMEM((n_rows, V_WIDTH), F32)],
        compiler_params=pltpu.CompilerParams(dimension_semantics=("arbitrary",),
                                             vmem_limit_bytes=VMEM_LIMIT),
        name="out_mlp_sample_attn",
    )(pool, attn, x, w_out, gf, w_up, w_down, gl, *lams, subln_g, qb, kb_new, vb_new, cache_kt, cache_v)


def _keys_from_feature_major(kt, frames):
    streams = kt.shape[0]
    return jnp.transpose(kt.reshape(streams, N_HEADS, 2, HEAD_DIM, frames), (0, 4, 1, 2, 3))[None]


def kernel(x_prompt, x_sample, state_pool, cache_k, cache_v, norm_mix_g, w_in, w_pool, pool_scale,
           lambda_q1, lambda_k1, lambda_q2, lambda_k2, subln_g, w_out, norm_ffn_g, w_up, w_down,
           norm_final_g):
    assert w_in.shape[0] == 1, "one layer per call"
    B, T, _ = x_prompt.shape
    S, TS, _ = x_sample.shape
    past_len = cache_k.shape[2]
    lambda_init = 0.8 - 0.6 * math.exp(-0.3 * 0)

    g_mix = norm_mix_g[0][None]
    g_ffn = norm_ffn_g[0][None]
    g_fin = norm_final_g[None]
    g_sub = subln_g[0][None]
    pscale = pool_scale[0][None]
    lams = (lambda_q1[0][None], lambda_k1[0][None], lambda_q2[0][None], lambda_k2[0][None])
    w_in_b = w_in[0].astype(BF16)
    w_pool_b = w_pool[0].astype(BF16)
    w_out_b = w_out[0].astype(BF16)
    w_up_b = w_up[0].astype(BF16)
    w_down_b = w_down[0].astype(BF16)

    zero_buf = jnp.zeros((B, TAIL, POOL_WIDTH), F32)
    kt_p, v_p, qb, kb, vtb, pool_p, tail_p = _proj_pool(
        x_prompt, zero_buf, g_mix, w_in_b, w_pool_b, pscale, pos0=0, bb=1, tm=PROMPT_ROWS,
        feature_major=True)
    attn_p = _prompt_attn(qb, kb, vtb, lams, g_sub.reshape(V_DIM, 1), lambda_init=lambda_init, blk=ATTN_BLOCK)

    buf_s = jnp.pad(state_pool[0], ((0, 0), (TAIL - POOL_STATE, 0), (0, 0)))
    k_s, v_s, qb_s, kb_s, vb_s, pool_s, tail_s = _proj_pool(
        x_sample, buf_s, g_mix, w_in_b, w_pool_b, pscale, pos0=past_len, bb=S, tm=TS,
        feature_major=False)
    cache_kt = jnp.transpose(cache_k[0], (0, 2, 3, 4, 1)).reshape(S, QK_WIDTH, past_len)
    cache_vi = cache_v[0].reshape(S, past_len * N_HEADS, V_DIM)

    y_p, attn_s = _out_mlp_with_sample_attn(
        pool_p.reshape(B * T, -1), attn_p.reshape(B * T, -1), x_prompt.reshape(B * T, -1),
        w_out_b, g_ffn, w_up_b, w_down_b, g_fin, qb_s, kb_s, vb_s, cache_kt, cache_vi, lams, g_sub,
        tm=PROMPT_ROWS, lambda_init=lambda_init)
    y_p = y_p.reshape(B, T, D_MODEL)
    y_s = _out_mlp(pool_s.reshape(S * TS, -1), attn_s.reshape(S * TS, -1), x_sample.reshape(S * TS, -1),
                   w_out_b, g_ffn, w_up_b, w_down_b, g_fin, tm=S * TS).reshape(S, TS, D_MODEL)

    return (y_p, y_s,
            tail_p[:, TAIL - POOL_STATE:][None],
            _keys_from_feature_major(kt_p, T),
            v_p.reshape(1, B, T, N_HEADS, V_DIM),
            tail_s[:, TAIL - POOL_STATE:][None],
            k_s.reshape(1, S, TS, N_HEADS, 2, HEAD_DIM),
            v_s.reshape(1, S, TS, N_HEADS, V_DIM))
```

```python
import functools
import math

import jax
import jax.numpy as jnp
from jax import lax
from jax.experimental import pallas as pl
from jax.experimental.pallas import tpu as pltpu

D_MODEL = 1024
POOL_WIDTH = 512
POOL_WINDOWS = (2, 4, 8, 16)
POOL_GROUP = 128
POOL_STATE = 15
N_HEADS = 4
HEAD_DIM = 64
V_DIM = 128
QK_WIDTH = 512
V_WIDTH = N_HEADS * V_DIM
D_FF = 4096
CHUNK_BITS = 6
HEAD_DIM_BITS = 6
assert HEAD_DIM == 1 << HEAD_DIM_BITS
EPS = 1e-6
SUBLN_EPS = 1e-5

TAIL = 16
HEAD_LANES = 2 * HEAD_DIM
SUM_ROWS = 16
VT_ROWS = V_DIM + SUM_ROWS
BIAS_TERMS = 3
BIAS_ROWS = 16
assert 3 * BIAS_TERMS <= BIAS_ROWS

BF16 = jnp.bfloat16
F32 = jnp.float32
LOG2_E = math.log2(math.e)

PROMPT_ROWS = 512
ATTN_BLOCK = 256
FF_CHUNK = 1024
VMEM_LIMIT = 56 * 1024 * 1024


def _nt_dot(a, b):
    return lax.dot_general(a, b, (((1,), (1,)), ((), ())), preferred_element_type=F32)


def _dot(a, b):
    return jnp.dot(a, b, preferred_element_type=F32)


def _head_slope(h):
    return lax.bitcast_convert_type((127 - 2 * (h + 1)) << 23, F32) * LOG2_E


def _lambda(lq1, lk1, lq2, lk2, lambda_init):
    return (jnp.exp(jnp.sum(lq1[...] * lk1[...], axis=-1, keepdims=True))
            - jnp.exp(jnp.sum(lq2[...] * lk2[...], axis=-1, keepdims=True)) + lambda_init)


def _sub_ln(o0, o1, lam, g, lambda_init):
    o = o0 - lam * o1
    ms = jnp.mean(o * o, axis=-1, keepdims=True)
    return (o * lax.rsqrt(ms + SUBLN_EPS) * g) * (1.0 - lambda_init)


def _proj_pool_kernel(x_ref, g_ref, w_in_ref, w_pool_ref, pscale_ref, buf_ref,
                      k_ref, v_ref, qb_ref, kb_ref, vb_ref, pool_ref, tail_ref, ext_ref,
                      *, pos0, feature_major):
    bb, tm, _ = x_ref.shape
    rows = bb * tm
    t = pl.program_id(1)

    x = x_ref[...].reshape(rows, D_MODEL)
    ms = jnp.mean(x * x, axis=-1, keepdims=True)
    hn = (x * lax.rsqrt(ms + EPS) * g_ref[...]).astype(BF16)

    def proj(j):
        return _dot(hn, w_in_ref[:, j * 512:(j + 1) * 512])

    @pl.when(t == 0)
    def _():
        ext_ref[:, 0:TAIL, :] = buf_ref[...]

    ext_ref[:, TAIL:TAIL + tm, :] = proj(0).reshape(bb, tm, POOL_WIDTH)
    pos = pos0 + t * tm + lax.broadcasted_iota(jnp.int32, (1, tm, 1), 1)

    def pool_group(g):
        w = POOL_WINDOWS[g]
        lanes = slice(g * POOL_GROUP, (g + 1) * POOL_GROUP)
        ext = ext_ref[:, :, lanes]
        acc, span = ext, 1
        while span < w:
            acc = acc + pltpu.roll(acc, span, axis=1)
            span *= 2
        cnt = jnp.minimum(pos + 1, w).astype(F32)
        diff = (acc[:, TAIL:, :] / cnt - ext[:, TAIL:, :]).reshape(rows, POOL_GROUP).astype(BF16)
        y = _dot(diff, w_pool_ref[g]) * pscale_ref[:, lanes]
        pool_ref[:, :, lanes] = y.astype(BF16).reshape(bb, tm, POOL_GROUP)

    q = proj(1)
    qb_ref[...] = (q * (LOG2_E / math.sqrt(HEAD_DIM))).astype(BF16).reshape(bb, tm, QK_WIDTH)
    pool_group(0)
    pool_group(1)
    k = proj(2)
    kb_ref[...] = k.astype(BF16).reshape(bb, tm, QK_WIDTH)
    if feature_major:
        assert bb == 1
        k_ref[0] = k.T
    else:
        k_ref[...] = k.reshape(bb, tm, QK_WIDTH)
    pool_group(2)
    pool_group(3)
    v = proj(3)
    if feature_major:
        vt = v.T.astype(BF16)
        for h in range(N_HEADS):
            vb_ref[0, h * VT_ROWS:h * VT_ROWS + V_DIM, :] = vt[h * V_DIM:(h + 1) * V_DIM]
            vb_ref[0, h * VT_ROWS + V_DIM:(h + 1) * VT_ROWS, :] = jnp.ones((SUM_ROWS, tm), BF16)
    else:
        vb_ref[...] = v.astype(BF16).reshape(bb, tm, V_WIDTH)
    for h in range(N_HEADS):
        v_ref[:, pl.ds(h, tm, stride=N_HEADS), :] = v[:, h * V_DIM:(h + 1) * V_DIM].reshape(bb, tm, V_DIM)

    tail = ext_ref[:, tm:tm + TAIL, :]
    tail_ref[...] = tail
    ext_ref[:, 0:TAIL, :] = tail


def _proj_pool(x, buf, g, w_in, w_pool, pscale, *, pos0, bb, tm, feature_major):
    B, T, _ = x.shape
    grid = (B // bb, T // tm)
    row_spec = lambda width: pl.BlockSpec((bb, tm, width), lambda b, t: (b, t, 0))
    const = lambda shape: pl.BlockSpec(shape, lambda b, t: (0,) * len(shape), pipeline_mode=pl.Buffered(1))
    seq_spec = pl.BlockSpec((bb, TAIL, POOL_WIDTH), lambda b, t: (b, 0, 0))
    act = lambda dtype: jax.ShapeDtypeStruct((B, T, QK_WIDTH), dtype)
    if feature_major:
        fm_spec = lambda rows: pl.BlockSpec((bb, rows, tm), lambda b, t: (b, 0, t))
        fm_shape = lambda rows, dtype: jax.ShapeDtypeStruct((B, rows, T), dtype)
        k_spec, k_shape = fm_spec(QK_WIDTH), fm_shape(QK_WIDTH, F32)
        vb_spec, vb_shape = fm_spec(N_HEADS * VT_ROWS), fm_shape(N_HEADS * VT_ROWS, BF16)
    else:
        k_spec, k_shape = row_spec(QK_WIDTH), act(F32)
        vb_spec, vb_shape = row_spec(V_WIDTH), act(BF16)
    return pl.pallas_call(
        functools.partial(_proj_pool_kernel, pos0=pos0, feature_major=feature_major),
        grid=grid,
        in_specs=[row_spec(D_MODEL), const((1, D_MODEL)), const((D_MODEL, 4 * 512)),
                  const((len(POOL_WINDOWS), POOL_GROUP, POOL_GROUP)), const((1, POOL_WIDTH)), seq_spec],
        out_specs=[k_spec, pl.BlockSpec((bb, tm * N_HEADS, V_DIM), lambda b, t: (b, t, 0)),
                   row_spec(QK_WIDTH), row_spec(QK_WIDTH), vb_spec, row_spec(POOL_WIDTH), seq_spec],
        out_shape=[k_shape, jax.ShapeDtypeStruct((B, T * N_HEADS, V_DIM), F32),
                   act(BF16), act(BF16), vb_shape, act(BF16),
                   jax.ShapeDtypeStruct((B, TAIL, POOL_WIDTH), F32)],
        scratch_shapes=[pltpu.VMEM((bb, TAIL + tm, POOL_WIDTH), F32)],
        compiler_params=pltpu.CompilerParams(dimension_semantics=("arbitrary", "arbitrary"),
                                             vmem_limit_bytes=VMEM_LIMIT),
        name="proj_pool",
    )(x, g, w_in, w_pool, pscale, buf)


def _prompt_attn_kernel(lq1, lk1, lq2, lk2, gcol_ref, q_ref, k_ref, vt_ref, o_ref,
                        key_pos_ref, bias_diag_ref, s_even_ref, s_odd_ref, *, lambda_init, blk):
    n_blocks = q_ref.shape[1] // blk
    h = pl.program_id(0)
    slope = _head_slope(jnp.full((1, 1), h, jnp.int32))

    @pl.when((h == 0) & (pl.program_id(1) == 0))
    def _():
        kg = lax.broadcasted_iota(jnp.int32, key_pos_ref.shape, 0)
        lane = lax.broadcasted_iota(jnp.int32, key_pos_ref.shape, 1)
        code = jnp.where(lane < BIAS_TERMS, kg >> 8,
                         jnp.where(lane < 2 * BIAS_TERMS, kg & 255, jnp.where(lane < 3 * BIAS_TERMS, 1, 0)))
        key_pos_ref[...] = code.astype(F32).astype(BF16)

    @pl.when(pl.program_id(1) == 0)
    def _():
        kk = lax.broadcasted_iota(jnp.int32, bias_diag_ref.shape, 0)
        qq = lax.broadcasted_iota(jnp.int32, bias_diag_ref.shape, 1) & (blk - 1)
        visible = (kk >> CHUNK_BITS) <= (qq >> CHUNK_BITS)
        bias_diag_ref[...] = jnp.where(visible, -slope * jnp.abs(qq - kk).astype(F32), -jnp.inf)

    lam = _lambda(lq1, lk1, lq2, lk2, lambda_init)
    feat = lax.broadcasted_iota(jnp.int32, (HEAD_LANES, blk), 0)
    zero = jnp.zeros((HEAD_LANES, blk), BF16)
    q_off = (lax.broadcasted_iota(jnp.int32, (1, 2 * blk), 1) & (blk - 1)).astype(F32)
    bias_row = lax.broadcasted_iota(jnp.int32, (BIAS_ROWS, 2 * blk), 0)

    def pieces(x):
        out = []
        for _ in range(BIAS_TERMS):
            piece = x.astype(BF16).astype(F32)
            out.append(piece)
            x = x - piece
        return out

    s_slot = (s_even_ref, s_odd_ref)
    run_time_zero = jnp.minimum(pl.program_id(1), 0)

    def sublane_max(s):
        return jnp.max(s.reshape(blk // 8, 8, s.shape[1]), axis=0)

    def scores(c, col_max):
        n = c * blk
        qt = q_ref[0, n:n + blk, :].astype(F32).T.astype(BF16)
        q_cols = jnp.concatenate([jnp.where(feat < HEAD_DIM, qt, zero), jnp.where(feat >= HEAD_DIM, qt, zero)],
                                 axis=1)
        s_diag = _dot(k_ref[0, n:n + blk, :], q_cols) + bias_diag_ref[...]
        s_slot[c % 2][n:n + blk, :] = s_diag
        m = sublane_max(s_diag)
        if c > 0:
            terms = pieces(slope * 256.0) + pieces(slope) + pieces(-slope * (q_off + float(n)))
            bias_cols = jnp.zeros(bias_row.shape, F32)
            for r, term in enumerate(terms):
                bias_cols = jnp.where(bias_row == r, term, bias_cols)
            w = jnp.concatenate([q_cols, bias_cols.astype(BF16),
                                 jnp.zeros((HEAD_LANES - BIAS_ROWS, 2 * blk), BF16)], axis=0)
            for j in range(c):
                rows = slice(j * blk, (j + 1) * blk)
                s_past = _dot(jnp.concatenate([k_ref[0, rows, :], key_pos_ref[rows, :]], axis=1), w)
                s_slot[c % 2][rows, :] = s_past
                m = jnp.maximum(m, sublane_max(s_past))
        col_max[c] = jnp.max(m, axis=0, keepdims=True)

    def outputs(c, m):
        n = c * blk
        acc = None
        for j in range(c + 1):
            rows = slice(j * blk, (j + 1) * blk)
            s = s_slot[c % 2][pl.ds(pl.multiple_of(run_time_zero + j * blk, blk), blk), :]
            pv = _dot(vt_ref[0, :, rows], jnp.exp2(s - m).astype(BF16))
            acc = pv if acc is None else acc + pv
        o = [acc[0:V_DIM, half * blk:(half + 1) * blk] / acc[V_DIM:V_DIM + 1, half * blk:(half + 1) * blk]
             for half in range(2)]
        ot = o[0] - lam * o[1]
        ms = jnp.mean(ot * ot, axis=0, keepdims=True)
        yt = (ot * lax.rsqrt(ms + SUBLN_EPS) * gcol_ref[...]) * (1.0 - lambda_init)
        o_ref[0, n:n + blk, :] = yt.T.astype(o_ref.dtype)

    col_max = {}
    scores(n_blocks - 1, col_max)
    for c in reversed(range(n_blocks)):
        if c > 0:
            scores(c - 1, col_max)
        outputs(c, col_max[c])


def _prompt_attn(qb, kb, vtb, lams, subln_gcol, *, lambda_init, blk):
    B, T, _ = qb.shape
    grid = (N_HEADS, B)
    vec = lambda n: pl.BlockSpec((1, n), lambda h, b: (0, 0))
    qk_spec = pl.BlockSpec((1, T, HEAD_LANES), lambda h, b: (b, 0, h))
    vt_spec = pl.BlockSpec((1, VT_ROWS, T), lambda h, b: (b, h, 0))
    return pl.pallas_call(
        functools.partial(_prompt_attn_kernel, lambda_init=lambda_init, blk=blk),
        grid=grid,
        in_specs=[vec(HEAD_DIM)] * 4 + [pl.BlockSpec((V_DIM, 1), lambda h, b: (0, 0)), qk_spec, qk_spec, vt_spec],
        out_specs=qk_spec,
        out_shape=jax.ShapeDtypeStruct((B, T, V_WIDTH), BF16),
        scratch_shapes=[pltpu.VMEM((T - blk, HEAD_LANES), BF16), pltpu.VMEM((blk, 2 * blk), F32),
                        pltpu.VMEM((T, 2 * blk), F32), pltpu.VMEM((T, 2 * blk), F32)],
        compiler_params=pltpu.CompilerParams(dimension_semantics=("arbitrary",) * 2,
                                             vmem_limit_bytes=VMEM_LIMIT),
        name="prompt_attn",
    )(*lams, subln_gcol, qb, kb, vtb)


def _sample_attn_step(j, nj, lq1, lk1, lq2, lk2, g_ref, q_ref, kn_ref, vn_ref, ckt_ref, cv_ref, o_ref,
                      qrows_ref, m_ref, l_ref, acc_ref, *, lambda_init, past_len):
    tq = q_ref.shape[1]
    head_rows = 2 * tq
    n_rows = N_HEADS * head_rows
    kb = ckt_ref.shape[2]

    tq_bits = tq.bit_length() - 1
    assert tq == 1 << tq_bits
    row = lax.broadcasted_iota(jnp.int32, (n_rows, 1), 0)
    q_rel = row & (tq - 1)
    slope = _head_slope(row >> (tq_bits + 1))

    @pl.when(j == 0)
    def _():
        q = q_ref[0]
        tiled = jnp.concatenate([q] * (2 * N_HEADS), axis=0)
        r = lax.broadcasted_iota(jnp.int32, tiled.shape, 0)
        c = lax.broadcasted_iota(jnp.int32, tiled.shape, 1)
        qrows_ref[...] = jnp.where((r >> tq_bits) == (c >> HEAD_DIM_BITS), tiled, jnp.zeros_like(tiled))
        m_ref[...] = jnp.full(m_ref.shape, -jnp.inf, F32)
        l_ref[...] = jnp.zeros(l_ref.shape, F32)
        acc_ref[...] = jnp.zeros(acc_ref.shape, F32)

    def update(s, values):
        m_old = m_ref[...]
        m_new = jnp.maximum(m_old, jnp.max(s, axis=-1, keepdims=True))
        p = jnp.exp2(s - m_new)
        alpha = jnp.exp2(m_old - m_new)
        l_ref[...] = alpha * l_ref[...] + jnp.sum(p, axis=-1, keepdims=True)
        m_ref[...] = m_new
        acc_ref[...] = alpha * acc_ref[...] + _dot(p.astype(BF16), values)

    scores = []

    def score_cache():
        k_rel = (j * kb - past_len + lax.broadcasted_iota(jnp.int32, (1, kb), 1)).astype(F32)
        scores.append(_dot(qrows_ref[...], ckt_ref[0].astype(BF16)) + slope * k_rel)

    def attend_cache():
        values = jnp.concatenate(
            [cv_ref[0, pl.ds(h, kb, stride=N_HEADS), :].astype(BF16) for h in range(N_HEADS)], axis=1)
        update(scores.pop(), values)

    def finish_stream():
        @pl.when(j == nj - 1)
        def _():
            k_new = lax.broadcasted_iota(jnp.int32, (1, tq), 1)
            update(_nt_dot(qrows_ref[...], kn_ref[0]) + slope * (q_rel - jnp.abs(q_rel - k_new)).astype(F32),
                   vn_ref[0])
            lam = _lambda(lq1, lk1, lq2, lk2, lambda_init)
            out = acc_ref[...] / l_ref[...]
            for h in range(N_HEADS):
                own = out[h * head_rows:(h + 1) * head_rows, h * V_DIM:(h + 1) * V_DIM]
                o_ref[0, :, h * V_DIM:(h + 1) * V_DIM] = _sub_ln(own[:tq], own[tq:], lam, g_ref[...],
                                                                 lambda_init).astype(o_ref.dtype)

    return score_cache, attend_cache, finish_stream


def _out_mlp_kernel(pool_ref, attn_ref, x_ref, w_out_ref, gf_ref, w_up_ref, w_down_ref, gl_ref, y_ref):
    _out_mlp_tile(pool_ref, attn_ref, x_ref, w_out_ref, gf_ref, w_up_ref, w_down_ref, gl_ref, y_ref)


def _out_mlp_tile(pool_ref, attn_ref, x_ref, w_out_ref, gf_ref, w_up_ref, w_down_ref, gl_ref, y_ref,
                  after_chunk=()):
    mixed = (_dot(pool_ref[...], w_out_ref[0:POOL_WIDTH, :]) + _dot(attn_ref[...], w_out_ref[POOL_WIDTH:, :]))
    h = x_ref[...] + mixed
    ms = jnp.mean(h * h, axis=-1, keepdims=True)
    hn = (h * lax.rsqrt(ms + EPS) * gf_ref[...]).astype(BF16)
    y = h
    for c in range(D_FF // FF_CHUNK):
        cols = slice(c * FF_CHUNK, (c + 1) * FF_CHUNK)
        a = jnp.maximum(_dot(hn, w_up_ref[:, cols]), 0.0)
        y = y + _dot((a * a).astype(BF16), w_down_ref[cols, :])
        if c < len(after_chunk):
            after_chunk[c]()
    ms = jnp.mean(y * y, axis=-1, keepdims=True)
    y_ref[...] = y * lax.rsqrt(ms + EPS) * gl_ref[...]


def _out_mlp(pool, attn, x, w_out, gf, w_up, w_down, gl, *, tm):
    rows = x.shape[0]
    row_spec = lambda width: pl.BlockSpec((tm, width), lambda r: (r, 0))
    const = lambda shape: pl.BlockSpec(shape, lambda r: (0, 0), pipeline_mode=pl.Buffered(1))
    return pl.pallas_call(
        _out_mlp_kernel,
        grid=(rows // tm,),
        in_specs=[row_spec(POOL_WIDTH), row_spec(V_WIDTH), row_spec(D_MODEL),
                  const((D_MODEL, D_MODEL)), const((1, D_MODEL)), const((D_MODEL, D_FF)),
                  const((D_FF, D_MODEL)), const((1, D_MODEL))],
        out_specs=row_spec(D_MODEL),
        out_shape=jax.ShapeDtypeStruct((rows, D_MODEL), F32),
        compiler_params=pltpu.CompilerParams(dimension_semantics=("arbitrary",),
                                             vmem_limit_bytes=VMEM_LIMIT),
        name="out_mlp",
    )(pool, attn, x, w_out, gf, w_up, w_down, gl)


def _out_mlp_sample_attn_kernel(*refs, parts, lambda_init, past_len):
    mlp_in, samp_in, (y_ref, o_ref), scratch = refs[:8], refs[8:18], refs[18:20], refs[20:]
    r = pl.program_id(0)
    score_cache, attend_cache, finish_stream = _sample_attn_step(
        r % parts, parts, *samp_in, o_ref, *scratch, lambda_init=lambda_init, past_len=past_len)
    _out_mlp_tile(*mlp_in, y_ref, after_chunk=(score_cache, attend_cache))
    finish_stream()


def _out_mlp_with_sample_attn(pool, attn, x, w_out, gf, w_up, w_down, gl,
                              qb, kb_new, vb_new, cache_kt, cache_v, lams, subln_g, *, tm, lambda_init):
    rows = x.shape[0]
    S, tq, _ = qb.shape
    past_len = cache_kt.shape[2]
    steps = rows // tm
    assert steps % S == 0, "every stream gets the same number of grid steps"
    parts = steps // S
    kblk = past_len // parts
    n_rows = 2 * N_HEADS * tq
    row_spec = lambda width: pl.BlockSpec((tm, width), lambda r: (r, 0))
    const = lambda shape: pl.BlockSpec(shape, lambda r: (0, 0), pipeline_mode=pl.Buffered(1))
    vec = lambda n: pl.BlockSpec((1, n), lambda r: (0, 0))
    new_spec = pl.BlockSpec((1, tq, QK_WIDTH), lambda r: (r // parts, 0, 0))
    kt_spec = pl.BlockSpec((1, QK_WIDTH, kblk), lambda r: (r // parts, 0, r % parts))
    v_spec = pl.BlockSpec((1, kblk * N_HEADS, V_DIM), lambda r: (r // parts, r % parts, 0))
    return pl.pallas_call(
        functools.partial(_out_mlp_sample_attn_kernel, parts=parts, lambda_init=lambda_init, past_len=past_len),
        grid=(steps,),
        in_specs=[row_spec(POOL_WIDTH), row_spec(V_WIDTH), row_spec(D_MODEL),
                  const((D_MODEL, D_MODEL)), const((1, D_MODEL)), const((D_MODEL, D_FF)),
                  const((D_FF, D_MODEL)), const((1, D_MODEL))]
                 + [vec(HEAD_DIM)] * 4 + [vec(V_DIM), new_spec, new_spec, new_spec, kt_spec, v_spec],
        out_specs=[row_spec(D_MODEL), new_spec],
        out_shape=[jax.ShapeDtypeStruct((rows, D_MODEL), F32), jax.ShapeDtypeStruct((S, tq, V_WIDTH), BF16)],
        scratch_shapes=[pltpu.VMEM((n_rows, QK_WIDTH), BF16), pltpu.VMEM((n_rows, 1), F32),
                        pltpu.VMEM((n_rows, 1), F32), pltpu.VMEM((n_rows, V_WIDTH), F32)],
        compiler_params=pltpu.CompilerParams(dimension_semantics=("arbitrary",),
                                             vmem_limit_bytes=VMEM_LIMIT),
        name="out_mlp_sample_attn",
    )(pool, attn, x, w_out, gf, w_up, w_down, gl, *lams, subln_g, qb, kb_new, vb_new, cache_kt, cache_v)


def _keys_from_feature_major(kt, frames):
    streams = kt.shape[0]
    return jnp.transpose(kt.reshape(streams, N_HEADS, 2, HEAD_DIM, frames), (0, 4, 1, 2, 3))[None]


def kernel(x_prompt, x_sample, state_pool, cache_k, cache_v, norm_mix_g, w_in, w_pool, pool_scale,
           lambda_q1, lambda_k1, lambda_q2, lambda_k2, subln_g, w_out, norm_ffn_g, w_up, w_down,
           norm_final_g):
    assert w_in.shape[0] == 1, "one layer per call"
    B, T, _ = x_prompt.shape
    S, TS, _ = x_sample.shape
    past_len = cache_k.shape[2]
    lambda_init = 0.8 - 0.6 * math.exp(-0.3 * 0)

    g_mix = norm_mix_g[0][None]
    g_ffn = norm_ffn_g[0][None]
    g_fin = norm_final_g[None]
    g_sub = subln_g[0][None]
    pscale = pool_scale[0][None]
    lams = (lambda_q1[0][None], lambda_k1[0][None], lambda_q2[0][None], lambda_k2[0][None])
    w_in_b = w_in[0].astype(BF16)
    w_pool_b = w_pool[0].astype(BF16)
    w_out_b = w_out[0].astype(BF16)
    w_up_b = w_up[0].astype(BF16)
    w_down_b = w_down[0].astype(BF16)

    zero_buf = jnp.zeros((B, TAIL, POOL_WIDTH), F32)
    kt_p, v_p, qb, kb, vtb, pool_p, tail_p = _proj_pool(
        x_prompt, zero_buf, g_mix, w_in_b, w_pool_b, pscale, pos0=0, bb=1, tm=PROMPT_ROWS,
        feature_major=True)
    attn_p = _prompt_attn(qb, kb, vtb, lams, g_sub.reshape(V_DIM, 1), lambda_init=lambda_init, blk=ATTN_BLOCK)

    buf_s = jnp.pad(state_pool[0], ((0, 0), (TAIL - POOL_STATE, 0), (0, 0)))
    k_s, v_s, qb_s, kb_s, vb_s, pool_s, tail_s = _proj_pool(
        x_sample, buf_s, g_mix, w_in_b, w_pool_b, pscale, pos0=past_len, bb=S, tm=TS,
        feature_major=False)
    cache_kt = jnp.transpose(cache_k[0], (0, 2, 3, 4, 1)).reshape(S, QK_WIDTH, past_len)
    cache_vi = cache_v[0].reshape(S, past_len * N_HEADS, V_DIM)

    y_p, attn_s = _out_mlp_with_sample_attn(
        pool_p.reshape(B * T, -1), attn_p.reshape(B * T, -1), x_prompt.reshape(B * T, -1),
        w_out_b, g_ffn, w_up_b, w_down_b, g_fin, qb_s, kb_s, vb_s, cache_kt, cache_vi, lams, g_sub,
        tm=PROMPT_ROWS, lambda_init=lambda_init)
    y_p = y_p.reshape(B, T, D_MODEL)
    y_s = _out_mlp(pool_s.reshape(S * TS, -1), attn_s.reshape(S * TS, -1), x_sample.reshape(S * TS, -1),
                   w_out_b, g_ffn, w_up_b, w_down_b, g_fin, tm=S * TS).reshape(S, TS, D_MODEL)

    return (y_p, y_s,
            tail_p[:, TAIL - POOL_STATE:][None],
            _keys_from_feature_major(kt_p, T),
            v_p.reshape(1, B, T, N_HEADS, V_DIM),
            tail_s[:, TAIL - POOL_STATE:][None],
            k_s.reshape(1, S, TS, N_HEADS, 2, HEAD_DIM),
            v_s.reshape(1, S, TS, N_HEADS, V_DIM))
```

```python
import functools
import math

import jax
import jax.numpy as jnp
from jax import lax
from jax.experimental import pallas as pl
from jax.experimental.pallas import tpu as pltpu

D_MODEL = 1024
POOL_WIDTH = 512
POOL_WINDOWS = (2, 4, 8, 16)
POOL_GROUP = 128
POOL_STATE = 15
N_HEADS = 4
HEAD_DIM = 64
V_DIM = 128
QK_WIDTH = 512
V_WIDTH = N_HEADS * V_DIM
D_FF = 4096
CHUNK_BITS = 6
HEAD_DIM_BITS = 6
assert HEAD_DIM == 1 << HEAD_DIM_BITS
EPS = 1e-6
SUBLN_EPS = 1e-5

TAIL = 16
HEAD_LANES = 2 * HEAD_DIM
SUM_ROWS = 16
VT_ROWS = V_DIM + SUM_ROWS
BIAS_TERMS = 3
BIAS_ROWS = 16
assert 3 * BIAS_TERMS <= BIAS_ROWS

BF16 = jnp.bfloat16
F32 = jnp.float32
LOG2_E = math.log2(math.e)

PROMPT_ROWS = 512
ATTN_BLOCK = 256
FF_CHUNK = 1024
INTERLEAVE_GROUP = 1
VMEM_LIMIT = 56 * 1024 * 1024


def _nt_dot(a, b):
    return lax.dot_general(a, b, (((1,), (1,)), ((), ())), preferred_element_type=F32)


def _dot(a, b):
    return jnp.dot(a, b, preferred_element_type=F32)


def _head_slope(h):
    return lax.bitcast_convert_type((127 - 2 * (h + 1)) << 23, F32) * LOG2_E


def _lambda(lq1, lk1, lq2, lk2, lambda_init):
    return (jnp.exp(jnp.sum(lq1[...] * lk1[...], axis=-1, keepdims=True))
            - jnp.exp(jnp.sum(lq2[...] * lk2[...], axis=-1, keepdims=True)) + lambda_init)


def _sub_ln(o0, o1, lam, g, lambda_init):
    o = o0 - lam * o1
    ms = jnp.mean(o * o, axis=-1, keepdims=True)
    return (o * lax.rsqrt(ms + SUBLN_EPS) * g) * (1.0 - lambda_init)


def _proj_pool_kernel(x_ref, g_ref, w_in_ref, w_pool_ref, pscale_ref, buf_ref,
                      k_ref, v_ref, qb_ref, kb_ref, vb_ref, pool_ref, tail_ref, ext_ref,
                      *, pos0, feature_major):
    bb, tm, _ = x_ref.shape
    rows = bb * tm
    t = pl.program_id(1)

    @pl.when(t == 0)
    def _():
        ext_ref[:, 0:TAIL, :] = buf_ref[...]

    x = x_ref[...].reshape(rows, D_MODEL)
    ms = jnp.mean(x * x, axis=-1, keepdims=True)
    hn = (x * lax.rsqrt(ms + EPS) * g_ref[...]).astype(BF16)

    def proj(j):
        return _dot(hn, w_in_ref[:, j * 512:(j + 1) * 512])

    ext_ref[:, TAIL:TAIL + tm, :] = proj(0).reshape(bb, tm, POOL_WIDTH)
    pos = pos0 + t * tm + lax.broadcasted_iota(jnp.int32, (1, tm, 1), 1)

    def pool_group(g):
        w = POOL_WINDOWS[g]
        lanes = slice(g * POOL_GROUP, (g + 1) * POOL_GROUP)
        ext = ext_ref[:, :, lanes]
        acc, span = ext, 1
        while span < w:
            acc = acc + pltpu.roll(acc, span, axis=1)
            span *= 2
        cnt = jnp.minimum(pos + 1, w).astype(F32)
        diff = (acc[:, TAIL:, :] / cnt - ext[:, TAIL:, :]).reshape(rows, POOL_GROUP).astype(BF16)
        y = _dot(diff, w_pool_ref[g]) * pscale_ref[:, lanes]
        pool_ref[:, :, lanes] = y.astype(BF16).reshape(bb, tm, POOL_GROUP)

    k = proj(2)
    kb_ref[...] = k.astype(BF16).reshape(bb, tm, QK_WIDTH)
    if feature_major:
        assert bb == 1
        k_ref[0] = k.T
    else:
        k_ref[...] = k.reshape(bb, tm, QK_WIDTH)
    pool_group(0)
    pool_group(1)
    v = proj(3)
    if feature_major:
        vt = v.T.astype(BF16)
        for h in range(N_HEADS):
            vb_ref[0, h * VT_ROWS:h * VT_ROWS + V_DIM, :] = vt[h * V_DIM:(h + 1) * V_DIM]
            vb_ref[0, h * VT_ROWS + V_DIM:(h + 1) * VT_ROWS, :] = jnp.ones((SUM_ROWS, tm), BF16)
    else:
        vb_ref[...] = v.astype(BF16).reshape(bb, tm, V_WIDTH)
    for h in range(N_HEADS):
        v_ref[:, pl.ds(h, tm, stride=N_HEADS), :] = v[:, h * V_DIM:(h + 1) * V_DIM].reshape(bb, tm, V_DIM)
    pool_group(2)
    pool_group(3)
    q = proj(1)
    qb_ref[...] = (q * (LOG2_E / math.sqrt(HEAD_DIM))).astype(BF16).reshape(bb, tm, QK_WIDTH)

    tail = ext_ref[:, tm:tm + TAIL, :]
    tail_ref[...] = tail
    ext_ref[:, 0:TAIL, :] = tail


def _proj_pool(x, buf, g, w_in, w_pool, pscale, *, pos0, bb, tm, feature_major):
    B, T, _ = x.shape
    grid = (B // bb, T // tm)
    row_spec = lambda width: pl.BlockSpec((bb, tm, width), lambda b, t: (b, t, 0))
    const = lambda shape: pl.BlockSpec(shape, lambda b, t: (0,) * len(shape), pipeline_mode=pl.Buffered(1))
    seq_spec = pl.BlockSpec((bb, TAIL, POOL_WIDTH), lambda b, t: (b, 0, 0))
    act = lambda dtype: jax.ShapeDtypeStruct((B, T, QK_WIDTH), dtype)
    if feature_major:
        fm_spec = lambda rows: pl.BlockSpec((bb, rows, tm), lambda b, t: (b, 0, t))
        fm_shape = lambda rows, dtype: jax.ShapeDtypeStruct((B, rows, T), dtype)
        k_spec, k_shape = fm_spec(QK_WIDTH), fm_shape(QK_WIDTH, F32)
        vb_spec, vb_shape = fm_spec(N_HEADS * VT_ROWS), fm_shape(N_HEADS * VT_ROWS, BF16)
    else:
        k_spec, k_shape = row_spec(QK_WIDTH), act(F32)
        vb_spec, vb_shape = row_spec(V_WIDTH), act(BF16)
    return pl.pallas_call(
        functools.partial(_proj_pool_kernel, pos0=pos0, feature_major=feature_major),
        grid=grid,
        in_specs=[row_spec(D_MODEL), const((1, D_MODEL)), const((D_MODEL, 4 * 512)),
                  const((len(POOL_WINDOWS), POOL_GROUP, POOL_GROUP)), const((1, POOL_WIDTH)), seq_spec],
        out_specs=[k_spec, pl.BlockSpec((bb, tm * N_HEADS, V_DIM), lambda b, t: (b, t, 0)),
                   row_spec(QK_WIDTH), row_spec(QK_WIDTH), vb_spec, row_spec(POOL_WIDTH), seq_spec],
        out_shape=[k_shape, jax.ShapeDtypeStruct((B, T * N_HEADS, V_DIM), F32),
                   act(BF16), act(BF16), vb_shape, act(BF16),
                   jax.ShapeDtypeStruct((B, TAIL, POOL_WIDTH), F32)],
        scratch_shapes=[pltpu.VMEM((bb, TAIL + tm, POOL_WIDTH), F32)],
        compiler_params=pltpu.CompilerParams(dimension_semantics=("arbitrary", "arbitrary"),
                                             vmem_limit_bytes=VMEM_LIMIT),
        name="proj_pool",
    )(x, g, w_in, w_pool, pscale, buf)


def _prompt_attn_kernel(lq1, lk1, lq2, lk2, gcol_ref, q_ref, k_ref, vt_ref, o_ref,
                        key_pos_ref, bias_diag_ref, s_even_ref, s_odd_ref, *, lambda_init, blk):
    n_blocks = q_ref.shape[1] // blk
    h = pl.program_id(0)
    slope = _head_slope(jnp.full((1, 1), h, jnp.int32))

    @pl.when((h == 0) & (pl.program_id(1) == 0))
    def _():
        kg = lax.broadcasted_iota(jnp.int32, key_pos_ref.shape, 0)
        lane = lax.broadcasted_iota(jnp.int32, key_pos_ref.shape, 1)
        code = jnp.where(lane < BIAS_TERMS, kg >> 8,
                         jnp.where(lane < 2 * BIAS_TERMS, kg & 255, jnp.where(lane < 3 * BIAS_TERMS, 1, 0)))
        key_pos_ref[...] = code.astype(F32).astype(BF16)

    @pl.when(pl.program_id(1) == 0)
    def _():
        kk = lax.broadcasted_iota(jnp.int32, bias_diag_ref.shape, 0)
        qq = lax.broadcasted_iota(jnp.int32, bias_diag_ref.shape, 1) & (blk - 1)
        visible = (kk >> CHUNK_BITS) <= (qq >> CHUNK_BITS)
        bias_diag_ref[...] = jnp.where(visible, -slope * jnp.abs(qq - kk).astype(F32), -jnp.inf)

    lam = _lambda(lq1, lk1, lq2, lk2, lambda_init)
    feat = lax.broadcasted_iota(jnp.int32, (HEAD_LANES, blk), 0)
    zero = jnp.zeros((HEAD_LANES, blk), BF16)
    q_off = (lax.broadcasted_iota(jnp.int32, (1, 2 * blk), 1) & (blk - 1)).astype(F32)
    bias_row = lax.broadcasted_iota(jnp.int32, (BIAS_ROWS, 2 * blk), 0)

    def pieces(x):
        out = []
        for _ in range(BIAS_TERMS):
            piece = x.astype(BF16).astype(F32)
            out.append(piece)
            x = x - piece
        return out

    s_slot = (s_even_ref, s_odd_ref)
    run_time_zero = jnp.minimum(pl.program_id(1), 0)

    def sublane_max(s):
        return jnp.max(s.reshape(blk // 8, 8, s.shape[1]), axis=0)

    def scores(c, col_max):
        n = c * blk
        qt = q_ref[0, n:n + blk, :].astype(F32).T.astype(BF16)
        q_cols = jnp.concatenate([jnp.where(feat < HEAD_DIM, qt, zero), jnp.where(feat >= HEAD_DIM, qt, zero)],
                                 axis=1)
        s_diag = _dot(k_ref[0, n:n + blk, :], q_cols) + bias_diag_ref[...]
        s_slot[c % 2][n:n + blk, :] = s_diag
        m = sublane_max(s_diag)
        yield
        if c > 0:
            terms = pieces(slope * 256.0) + pieces(slope) + pieces(-slope * (q_off + float(n)))
            bias_cols = jnp.zeros(bias_row.shape, F32)
            for r, term in enumerate(terms):
                bias_cols = jnp.where(bias_row == r, term, bias_cols)
            w = jnp.concatenate([q_cols, bias_cols.astype(BF16),
                                 jnp.zeros((HEAD_LANES - BIAS_ROWS, 2 * blk), BF16)], axis=0)
            for j in range(c):
                rows = slice(j * blk, (j + 1) * blk)
                s_past = _dot(jnp.concatenate([k_ref[0, rows, :], key_pos_ref[rows, :]], axis=1), w)
                s_slot[c % 2][rows, :] = s_past
                m = jnp.maximum(m, sublane_max(s_past))
                yield
        col_max[c] = jnp.max(m, axis=0, keepdims=True)

    def outputs(c, m):
        n = c * blk
        acc = None
        for j in range(c + 1):
            rows = slice(j * blk, (j + 1) * blk)
            s = s_slot[c % 2][pl.ds(pl.multiple_of(run_time_zero + j * blk, blk), blk), :]
            pv = _dot(vt_ref[0, :, rows], jnp.exp2(s - m).astype(BF16))
            acc = pv if acc is None else acc + pv
            yield
        o = [acc[0:V_DIM, half * blk:(half + 1) * blk] / acc[V_DIM:V_DIM + 1, half * blk:(half + 1) * blk]
             for half in range(2)]
        ot = o[0] - lam * o[1]
        ms = jnp.mean(ot * ot, axis=0, keepdims=True)
        yt = (ot * lax.rsqrt(ms + SUBLN_EPS) * gcol_ref[...]) * (1.0 - lambda_init)
        o_ref[0, n:n + blk, :] = yt.T.astype(o_ref.dtype)

    col_max = {}
    for _ in scores(n_blocks - 1, col_max):
        pass
    for c in reversed(range(n_blocks)):
        gens = [outputs(c, col_max[c])]
        if c > 0:
            gens.insert(0, scores(c - 1, col_max))
        while gens:
            for g in list(gens):
                for _ in range(INTERLEAVE_GROUP):
                    if next(g, StopIteration) is StopIteration:
                        gens.remove(g)
                        break


def _prompt_attn(qb, kb, vtb, lams, subln_gcol, *, lambda_init, blk):
    B, T, _ = qb.shape
    grid = (N_HEADS, B)
    vec = lambda n: pl.BlockSpec((1, n), lambda h, b: (0, 0))
    qk_spec = pl.BlockSpec((1, T, HEAD_LANES), lambda h, b: (b, 0, h))
    vt_spec = pl.BlockSpec((1, VT_ROWS, T), lambda h, b: (b, h, 0))
    return pl.pallas_call(
        functools.partial(_prompt_attn_kernel, lambda_init=lambda_init, blk=blk),
        grid=grid,
        in_specs=[vec(HEAD_DIM)] * 4 + [pl.BlockSpec((V_DIM, 1), lambda h, b: (0, 0)), qk_spec, qk_spec, vt_spec],
        out_specs=qk_spec,
        out_shape=jax.ShapeDtypeStruct((B, T, V_WIDTH), BF16),
        scratch_shapes=[pltpu.VMEM((T - blk, HEAD_LANES), BF16), pltpu.VMEM((blk, 2 * blk), F32),
                        pltpu.VMEM((T, 2 * blk), F32), pltpu.VMEM((T, 2 * blk), F32)],
        compiler_params=pltpu.CompilerParams(dimension_semantics=("arbitrary",) * 2,
                                             vmem_limit_bytes=VMEM_LIMIT),
        name="prompt_attn",
    )(*lams, subln_gcol, qb, kb, vtb)


def _sample_attn_step(j, nj, lq1, lk1, lq2, lk2, g_ref, q_ref, kn_ref, vn_ref, ckt_ref, cv_ref, o_ref,
                      qrows_ref, m_ref, l_ref, acc_ref, *, lambda_init, past_len):
    tq = q_ref.shape[1]
    head_rows = 2 * tq
    n_rows = N_HEADS * head_rows
    kb = ckt_ref.shape[2]

    tq_bits = tq.bit_length() - 1
    assert tq == 1 << tq_bits
    row = lax.broadcasted_iota(jnp.int32, (n_rows, 1), 0)
    q_rel = row & (tq - 1)
    slope = _head_slope(row >> (tq_bits + 1))

    @pl.when(j == 0)
    def _():
        q = q_ref[0]
        tiled = jnp.concatenate([q] * (2 * N_HEADS), axis=0)
        r = lax.broadcasted_iota(jnp.int32, tiled.shape, 0)
        c = lax.broadcasted_iota(jnp.int32, tiled.shape, 1)
        qrows_ref[...] = jnp.where((r >> tq_bits) == (c >> HEAD_DIM_BITS), tiled, jnp.zeros_like(tiled))
        m_ref[...] = jnp.full(m_ref.shape, -jnp.inf, F32)
        l_ref[...] = jnp.zeros(l_ref.shape, F32)
        acc_ref[...] = jnp.zeros(acc_ref.shape, F32)

    def update(s, values):
        m_old = m_ref[...]
        m_new = jnp.maximum(m_old, jnp.max(s, axis=-1, keepdims=True))
        p = jnp.exp2(s - m_new)
        alpha = jnp.exp2(m_old - m_new)
        l_ref[...] = alpha * l_ref[...] + jnp.sum(p, axis=-1, keepdims=True)
        m_ref[...] = m_new
        acc_ref[...] = alpha * acc_ref[...] + _dot(p.astype(BF16), values)

    scores = []

    def score_cache():
        k_rel = (j * kb - past_len + lax.broadcasted_iota(jnp.int32, (1, kb), 1)).astype(F32)
        scores.append(_dot(qrows_ref[...], ckt_ref[0].astype(BF16)) + slope * k_rel)

    def attend_cache():
        values = jnp.concatenate(
            [cv_ref[0, pl.ds(h, kb, stride=N_HEADS), :].astype(BF16) for h in range(N_HEADS)], axis=1)
        update(scores.pop(), values)

    def finish_stream():
        @pl.when(j == nj - 1)
        def _():
            k_new = lax.broadcasted_iota(jnp.int32, (1, tq), 1)
            update(_nt_dot(qrows_ref[...], kn_ref[0]) + slope * (q_rel - jnp.abs(q_rel - k_new)).astype(F32),
                   vn_ref[0])
            lam = _lambda(lq1, lk1, lq2, lk2, lambda_init)
            out = acc_ref[...] / l_ref[...]
            for h in range(N_HEADS):
                own = out[h * head_rows:(h + 1) * head_rows, h * V_DIM:(h + 1) * V_DIM]
                o_ref[0, :, h * V_DIM:(h + 1) * V_DIM] = _sub_ln(own[:tq], own[tq:], lam, g_ref[...],
                                                                 lambda_init).astype(o_ref.dtype)

    return score_cache, attend_cache, finish_stream


def _out_mlp_kernel(pool_ref, attn_ref, x_ref, w_out_ref, gf_ref, w_up_ref, w_down_ref, gl_ref, y_ref):
    _out_mlp_tile(pool_ref, attn_ref, x_ref, w_out_ref, gf_ref, w_up_ref, w_down_ref, gl_ref, y_ref)


def _out_mlp_tile(pool_ref, attn_ref, x_ref, w_out_ref, gf_ref, w_up_ref, w_down_ref, gl_ref, y_ref,
                  after_chunk=()):
    mixed = (_dot(pool_ref[...], w_out_ref[0:POOL_WIDTH, :]) + _dot(attn_ref[...], w_out_ref[POOL_WIDTH:, :]))
    h = x_ref[...] + mixed
    ms = jnp.mean(h * h, axis=-1, keepdims=True)
    hn = (h * lax.rsqrt(ms + EPS) * gf_ref[...]).astype(BF16)
    y = h
    for c in range(D_FF // FF_CHUNK):
        cols = slice(c * FF_CHUNK, (c + 1) * FF_CHUNK)
        a = jnp.maximum(_dot(hn, w_up_ref[:, cols]), 0.0)
        y = y + _dot((a * a).astype(BF16), w_down_ref[cols, :])
        if c < len(after_chunk):
            after_chunk[c]()
    ms = jnp.mean(y * y, axis=-1, keepdims=True)
    y_ref[...] = y * lax.rsqrt(ms + EPS) * gl_ref[...]


def _out_mlp(pool, attn, x, w_out, gf, w_up, w_down, gl, *, tm):
    rows = x.shape[0]
    row_spec = lambda width: pl.BlockSpec((tm, width), lambda r: (r, 0))
    const = lambda shape: pl.BlockSpec(shape, lambda r: (0, 0), pipeline_mode=pl.Buffered(1))
    return pl.pallas_call(
        _out_mlp_kernel,
        grid=(rows // tm,),
        in_specs=[row_spec(POOL_WIDTH), row_spec(V_WIDTH), row_spec(D_MODEL),
                  const((D_MODEL, D_MODEL)), const((1, D_MODEL)), const((D_MODEL, D_FF)),
                  const((D_FF, D_MODEL)), const((1, D_MODEL))],
        out_specs=row_spec(D_MODEL),
        out_shape=jax.ShapeDtypeStruct((rows, D_MODEL), F32),
        compiler_params=pltpu.CompilerParams(dimension_semantics=("arbitrary",),
                                             vmem_limit_bytes=VMEM_LIMIT),
        name="out_mlp",
    )(pool, attn, x, w_out, gf, w_up, w_down, gl)


def _out_mlp_sample_attn_kernel(*refs, parts, lambda_init, past_len):
    mlp_in, samp_in, (y_ref, o_ref), scratch = refs[:8], refs[8:18], refs[18:20], refs[20:]
    r = pl.program_id(0)
    score_cache, attend_cache, finish_stream = _sample_attn_step(
        r % parts, parts, *samp_in, o_ref, *scratch, lambda_init=lambda_init, past_len=past_len)
    _out_mlp_tile(*mlp_in, y_ref, after_chunk=(score_cache, attend_cache))
    finish_stream()


def _out_mlp_with_sample_attn(pool, attn, x, w_out, gf, w_up, w_down, gl,
                              qb, kb_new, vb_new, cache_kt, cache_v, lams, subln_g, *, tm, lambda_init):
    rows = x.shape[0]
    S, tq, _ = qb.shape
    past_len = cache_kt.shape[2]
    steps = rows // tm
    assert steps % S == 0, "every stream gets the same number of grid steps"
    parts = steps // S
    kblk = past_len // parts
    n_rows = 2 * N_HEADS * tq
    row_spec = lambda width: pl.BlockSpec((tm, width), lambda r: (r, 0))
    const = lambda shape: pl.BlockSpec(shape, lambda r: (0, 0), pipeline_mode=pl.Buffered(1))
    vec = lambda n: pl.BlockSpec((1, n), lambda r: (0, 0))
    new_spec = pl.BlockSpec((1, tq, QK_WIDTH), lambda r: (r // parts, 0, 0))
    kt_spec = pl.BlockSpec((1, QK_WIDTH, kblk), lambda r: (r // parts, 0, r % parts))
    v_spec = pl.BlockSpec((1, kblk * N_HEADS, V_DIM), lambda r: (r // parts, r % parts, 0))
    return pl.pallas_call(
        functools.partial(_out_mlp_sample_attn_kernel, parts=parts, lambda_init=lambda_init, past_len=past_len),
        grid=(steps,),
        in_specs=[row_spec(POOL_WIDTH), row_spec(V_WIDTH), row_spec(D_MODEL),
                  const((D_MODEL, D_MODEL)), const((1, D_MODEL)), const((D_MODEL, D_FF)),
                  const((D_FF, D_MODEL)), const((1, D_MODEL))]
                 + [vec(HEAD_DIM)] * 4 + [vec(V_DIM), new_spec, new_spec, new_spec, kt_spec, v_spec],
        out_specs=[row_spec(D_MODEL), new_spec],
        out_shape=[jax.ShapeDtypeStruct((rows, D_MODEL), F32), jax.ShapeDtypeStruct((S, tq, V_WIDTH), BF16)],
        scratch_shapes=[pltpu.VMEM((n_rows, QK_WIDTH), BF16), pltpu.VMEM((n_rows, 1), F32),
                        pltpu.VMEM((n_rows, 1), F32), pltpu.VMEM((n_rows, V_WIDTH), F32)],
        compiler_params=pltpu.CompilerParams(dimension_semantics=("arbitrary",),
                                             vmem_limit_bytes=VMEM_LIMIT),
        name="out_mlp_sample_attn",
    )(pool, attn, x, w_out, gf, w_up, w_down, gl, *lams, subln_g, qb, kb_new, vb_new, cache_kt, cache_v)


def _keys_from_feature_major(kt, frames):
    streams = kt.shape[0]
    return jnp.transpose(kt.reshape(streams, N_HEADS, 2, HEAD_DIM, frames), (0, 4, 1, 2, 3))[None]


def kernel(x_prompt, x_sample, state_pool, cache_k, cache_v, norm_mix_g, w_in, w_pool, pool_scale,
           lambda_q1, lambda_k1, lambda_q2, lambda_k2, subln_g, w_out, norm_ffn_g, w_up, w_down,
           norm_final_g):
    assert w_in.shape[0] == 1, "one layer per call"
    B, T, _ = x_prompt.shape
    S, TS, _ = x_sample.shape
    past_len = cache_k.shape[2]
    lambda_init = 0.8 - 0.6 * math.exp(-0.3 * 0)

    g_mix = norm_mix_g[0][None]
    g_ffn = norm_ffn_g[0][None]
    g_fin = norm_final_g[None]
    g_sub = subln_g[0][None]
    pscale = pool_scale[0][None]
    lams = (lambda_q1[0][None], lambda_k1[0][None], lambda_q2[0][None], lambda_k2[0][None])
    w_in_b = w_in[0].astype(BF16)
    w_pool_b = w_pool[0].astype(BF16)
    w_out_b = w_out[0].astype(BF16)
    w_up_b = w_up[0].astype(BF16)
    w_down_b = w_down[0].astype(BF16)

    zero_buf = jnp.zeros((B, TAIL, POOL_WIDTH), F32)
    kt_p, v_p, qb, kb, vtb, pool_p, tail_p = _proj_pool(
        x_prompt, zero_buf, g_mix, w_in_b, w_pool_b, pscale, pos0=0, bb=1, tm=PROMPT_ROWS,
        feature_major=True)
    attn_p = _prompt_attn(qb, kb, vtb, lams, g_sub.reshape(V_DIM, 1), lambda_init=lambda_init, blk=ATTN_BLOCK)

    buf_s = jnp.pad(state_pool[0], ((0, 0), (TAIL - POOL_STATE, 0), (0, 0)))
    k_s, v_s, qb_s, kb_s, vb_s, pool_s, tail_s = _proj_pool(
        x_sample, buf_s, g_mix, w_in_b, w_pool_b, pscale, pos0=past_len, bb=S, tm=TS,
        feature_major=False)
    cache_kt = jnp.transpose(cache_k[0], (0, 2, 3, 4, 1)).reshape(S, QK_WIDTH, past_len)
    cache_vi = cache_v[0].reshape(S, past_len * N_HEADS, V_DIM)

    y_p, attn_s = _out_mlp_with_sample_attn(
        pool_p.reshape(B * T, -1), attn_p.reshape(B * T, -1), x_prompt.reshape(B * T, -1),
        w_out_b, g_ffn, w_up_b, w_down_b, g_fin, qb_s, kb_s, vb_s, cache_kt, cache_vi, lams, g_sub,
        tm=PROMPT_ROWS, lambda_init=lambda_init)
    y_p = y_p.reshape(B, T, D_MODEL)
    y_s = _out_mlp(pool_s.reshape(S * TS, -1), attn_s.reshape(S * TS, -1), x_sample.reshape(S * TS, -1),
                   w_out_b, g_ffn, w_up_b, w_down_b, g_fin, tm=S * TS).reshape(S, TS, D_MODEL)

    return (y_p, y_s,
            tail_p[:, TAIL - POOL_STATE:][None],
            _keys_from_feature_major(kt_p, T),
            v_p.reshape(1, B, T, N_HEADS, V_DIM),
            tail_s[:, TAIL - POOL_STATE:][None],
            k_s.reshape(1, S, TS, N_HEADS, 2, HEAD_DIM),
            v_s.reshape(1, S, TS, N_HEADS, V_DIM))
```

```python
import functools
import math

import jax
import jax.numpy as jnp
from jax import lax
from jax.experimental import pallas as pl
from jax.experimental.pallas import tpu as pltpu

D_MODEL = 1024
POOL_WIDTH = 512
POOL_WINDOWS = (2, 4, 8, 16)
POOL_GROUP = 128
POOL_STATE = 15
N_HEADS = 4
HEAD_DIM = 64
V_DIM = 128
QK_WIDTH = 512
V_WIDTH = N_HEADS * V_DIM
D_FF = 4096
CHUNK_BITS = 6
HEAD_DIM_BITS = 6
assert HEAD_DIM == 1 << HEAD_DIM_BITS
EPS = 1e-6
SUBLN_EPS = 1e-5

TAIL = 16
HEAD_LANES = 2 * HEAD_DIM
SUM_ROWS = 16
VT_ROWS = V_DIM + SUM_ROWS
BIAS_TERMS = 3
BIAS_ROWS = 16
assert 3 * BIAS_TERMS <= BIAS_ROWS
POS_LOW_BITS = 8

BF16 = jnp.bfloat16
F32 = jnp.float32
LOG2_E = math.log2(math.e)

PROMPT_ROWS = 512
ATTN_BLOCK = 256
FF_CHUNK = 1024
INTERLEAVE_GROUP = 1
VMEM_LIMIT = 56 * 1024 * 1024


def _nt_dot(a, b):
    return lax.dot_general(a, b, (((1,), (1,)), ((), ())), preferred_element_type=F32)


def _dot(a, b):
    return jnp.dot(a, b, preferred_element_type=F32)


def _head_slope(h):
    return lax.bitcast_convert_type((127 - 2 * (h + 1)) << 23, F32) * LOG2_E


def _lambda(lq1, lk1, lq2, lk2, lambda_init):
    return (jnp.exp(jnp.sum(lq1[...] * lk1[...], axis=-1, keepdims=True))
            - jnp.exp(jnp.sum(lq2[...] * lk2[...], axis=-1, keepdims=True)) + lambda_init)


def _sub_ln(o0, o1, lam, g, lambda_init):
    o = o0 - lam * o1
    ms = jnp.mean(o * o, axis=-1, keepdims=True)
    return (o * lax.rsqrt(ms + SUBLN_EPS) * g) * (1.0 - lambda_init)


def _proj_pool_kernel(x_ref, g_ref, w_in_ref, w_pool_ref, pscale_ref, buf_ref,
                      k_ref, v_ref, qb_ref, kb_ref, vb_ref, pool_ref, tail_ref, ext_ref,
                      *, pos0, feature_major):
    bb, tm, _ = x_ref.shape
    rows = bb * tm
    t = pl.program_id(1)

    @pl.when(t == 0)
    def _():
        ext_ref[:, 0:TAIL, :] = buf_ref[...]

    x = x_ref[...].reshape(rows, D_MODEL)
    ms = jnp.mean(x * x, axis=-1, keepdims=True)
    hn = (x * lax.rsqrt(ms + EPS) * g_ref[...]).astype(BF16)

    def proj(j):
        return _dot(hn, w_in_ref[:, j * 512:(j + 1) * 512].astype(BF16))

    ext_ref[:, TAIL:TAIL + tm, :] = proj(0).reshape(bb, tm, POOL_WIDTH)
    pos = pos0 + t * tm + lax.broadcasted_iota(jnp.int32, (1, tm, 1), 1)

    def pool_group(g):
        w = POOL_WINDOWS[g]
        lanes = slice(g * POOL_GROUP, (g + 1) * POOL_GROUP)
        ext = ext_ref[:, :, lanes]
        acc, span = ext, 1
        while span < w:
            acc = acc + pltpu.roll(acc, span, axis=1)
            span *= 2
        cnt = jnp.minimum(pos + 1, w).astype(F32)
        diff = (acc[:, TAIL:, :] / cnt - ext[:, TAIL:, :]).reshape(rows, POOL_GROUP).astype(BF16)
        y = _dot(diff, w_pool_ref[g].astype(BF16)) * pscale_ref[:, lanes]
        pool_ref[:, :, lanes] = y.astype(BF16).reshape(bb, tm, POOL_GROUP)

    k = proj(2)
    kb_ref[...] = k.astype(BF16).reshape(bb, tm, QK_WIDTH)
    if feature_major:
        assert bb == 1
        k_ref[0] = k.T
    else:
        k_ref[...] = k.reshape(bb, tm, QK_WIDTH)
    pool_group(0)
    pool_group(1)
    v = proj(3)
    if feature_major:
        vt = v.T.astype(BF16)
        for h in range(N_HEADS):
            vb_ref[0, h * VT_ROWS:h * VT_ROWS + V_DIM, :] = vt[h * V_DIM:(h + 1) * V_DIM]
            vb_ref[0, h * VT_ROWS + V_DIM:(h + 1) * VT_ROWS, :] = jnp.ones((SUM_ROWS, tm), BF16)
    else:
        vb_ref[...] = v.astype(BF16).reshape(bb, tm, V_WIDTH)
    for h in range(N_HEADS):
        v_ref[:, pl.ds(h, tm, stride=N_HEADS), :] = v[:, h * V_DIM:(h + 1) * V_DIM].reshape(bb, tm, V_DIM)
    pool_group(2)
    pool_group(3)
    q = proj(1)
    qb_ref[...] = (q * (LOG2_E / math.sqrt(HEAD_DIM))).astype(BF16).reshape(bb, tm, QK_WIDTH)

    tail = ext_ref[:, tm:tm + TAIL, :]
    tail_ref[...] = tail
    ext_ref[:, 0:TAIL, :] = tail


def _proj_pool(x, buf, g, w_in, w_pool, pscale, *, pos0, bb, tm, feature_major):
    B, T, _ = x.shape
    grid = (B // bb, T // tm)
    row_spec = lambda width: pl.BlockSpec((bb, tm, width), lambda b, t: (b, t, 0))
    const = lambda shape: pl.BlockSpec(shape, lambda b, t: (0,) * len(shape), pipeline_mode=pl.Buffered(1))
    seq_spec = pl.BlockSpec((bb, TAIL, POOL_WIDTH), lambda b, t: (b, 0, 0))
    act = lambda dtype: jax.ShapeDtypeStruct((B, T, QK_WIDTH), dtype)
    if feature_major:
        fm_spec = lambda rows: pl.BlockSpec((bb, rows, tm), lambda b, t: (b, 0, t))
        fm_shape = lambda rows, dtype: jax.ShapeDtypeStruct((B, rows, T), dtype)
        k_spec, k_shape = fm_spec(QK_WIDTH), fm_shape(QK_WIDTH, F32)
        vb_spec, vb_shape = fm_spec(N_HEADS * VT_ROWS), fm_shape(N_HEADS * VT_ROWS, BF16)
    else:
        k_spec, k_shape = row_spec(QK_WIDTH), act(F32)
        vb_spec, vb_shape = row_spec(V_WIDTH), act(BF16)
    return pl.pallas_call(
        functools.partial(_proj_pool_kernel, pos0=pos0, feature_major=feature_major),
        grid=grid,
        in_specs=[row_spec(D_MODEL), const((1, D_MODEL)), const((D_MODEL, 4 * 512)),
                  const((len(POOL_WINDOWS), POOL_GROUP, POOL_GROUP)), const((1, POOL_WIDTH)), seq_spec],
        out_specs=[k_spec, pl.BlockSpec((bb, tm * N_HEADS, V_DIM), lambda b, t: (b, t, 0)),
                   row_spec(QK_WIDTH), row_spec(QK_WIDTH), vb_spec, row_spec(POOL_WIDTH), seq_spec],
        out_shape=[k_shape, jax.ShapeDtypeStruct((B, T * N_HEADS, V_DIM), F32),
                   act(BF16), act(BF16), vb_shape, act(BF16),
                   jax.ShapeDtypeStruct((B, TAIL, POOL_WIDTH), F32)],
        scratch_shapes=[pltpu.VMEM((bb, TAIL + tm, POOL_WIDTH), F32)],
        compiler_params=pltpu.CompilerParams(dimension_semantics=("arbitrary", "arbitrary"),
                                             vmem_limit_bytes=VMEM_LIMIT),
        name="proj_pool",
    )(x, g, w_in, w_pool, pscale, buf)


def _prompt_attn_kernel(lq1, lk1, lq2, lk2, gcol_ref, q_ref, k_ref, vt_ref, o_ref,
                        key_pos_ref, bias_diag_ref, s_even_ref, s_odd_ref, *, lambda_init, blk):
    n_blocks = q_ref.shape[1] // blk
    h = pl.program_id(0)
    slope = _head_slope(jnp.full((1, 1), h, jnp.int32))

    @pl.when((h == 0) & (pl.program_id(1) == 0))
    def _():
        kg = lax.broadcasted_iota(jnp.int32, key_pos_ref.shape, 0)
        lane = lax.broadcasted_iota(jnp.int32, key_pos_ref.shape, 1)
        code = jnp.where(lane < BIAS_TERMS, kg >> POS_LOW_BITS,
                         jnp.where(lane < 2 * BIAS_TERMS, kg & ((1 << POS_LOW_BITS) - 1),
                                   jnp.where(lane < 3 * BIAS_TERMS, 1, 0)))
        key_pos_ref[...] = code.astype(F32).astype(BF16)

    @pl.when(pl.program_id(1) == 0)
    def _():
        kk = lax.broadcasted_iota(jnp.int32, bias_diag_ref.shape, 0)
        qq = lax.broadcasted_iota(jnp.int32, bias_diag_ref.shape, 1) & (blk - 1)
        visible = (kk >> CHUNK_BITS) <= (qq >> CHUNK_BITS)
        bias_diag_ref[...] = jnp.where(visible, -slope * jnp.abs(qq - kk).astype(F32), -jnp.inf)

    lam = _lambda(lq1, lk1, lq2, lk2, lambda_init)
    feat = lax.broadcasted_iota(jnp.int32, (HEAD_LANES, blk), 0)
    zero = jnp.zeros((HEAD_LANES, blk), BF16)
    q_off = (lax.broadcasted_iota(jnp.int32, (1, 2 * blk), 1) & (blk - 1)).astype(F32)
    bias_row = lax.broadcasted_iota(jnp.int32, (BIAS_ROWS, 2 * blk), 0)

    def pieces(x):
        out = []
        for _ in range(BIAS_TERMS):
            piece = x.astype(BF16).astype(F32)
            out.append(piece)
            x = x - piece
        return out

    s_slot = (s_even_ref, s_odd_ref)
    run_time_zero = jnp.minimum(pl.program_id(1), 0)

    def sublane_max(s):
        return jnp.max(s.reshape(blk // 8, 8, s.shape[1]), axis=0)

    def scores(c, col_max):
        n = c * blk
        qt = q_ref[0, n:n + blk, :].astype(F32).T.astype(BF16)
        q_cols = jnp.concatenate([jnp.where(feat < HEAD_DIM, qt, zero), jnp.where(feat >= HEAD_DIM, qt, zero)],
                                 axis=1)
        s_diag = _dot(k_ref[0, n:n + blk, :], q_cols) + bias_diag_ref[...]
        s_slot[c % 2][n:n + blk, :] = s_diag
        m = sublane_max(s_diag)
        yield
        if c > 0:
            terms = (pieces(slope * float(1 << POS_LOW_BITS)) + pieces(slope)
                     + pieces(-slope * (q_off + float(n))))
            bias_cols = jnp.zeros(bias_row.shape, F32)
            for r, term in enumerate(terms):
                bias_cols = jnp.where(bias_row == r, term, bias_cols)
            w = jnp.concatenate([q_cols, bias_cols.astype(BF16),
                                 jnp.zeros((HEAD_LANES - BIAS_ROWS, 2 * blk), BF16)], axis=0)
            for j in range(c):
                rows = slice(j * blk, (j + 1) * blk)
                s_past = _dot(jnp.concatenate([k_ref[0, rows, :], key_pos_ref[rows, :]], axis=1), w)
                s_slot[c % 2][rows, :] = s_past
                m = jnp.maximum(m, sublane_max(s_past))
                yield
        col_max[c] = jnp.max(m, axis=0, keepdims=True)

    def outputs(c, m):
        n = c * blk
        acc = None
        for j in range(c + 1):
            rows = slice(j * blk, (j + 1) * blk)
            s = s_slot[c % 2][pl.ds(pl.multiple_of(run_time_zero + j * blk, blk), blk), :]
            pv = _dot(vt_ref[0, :, rows], jnp.exp2(s - m).astype(BF16))
            acc = pv if acc is None else acc + pv
            yield
        o = [acc[0:V_DIM, half * blk:(half + 1) * blk] / acc[V_DIM:V_DIM + 1, half * blk:(half + 1) * blk]
             for half in range(2)]
        ot = o[0] - lam * o[1]
        ms = jnp.mean(ot * ot, axis=0, keepdims=True)
        yt = (ot * lax.rsqrt(ms + SUBLN_EPS) * gcol_ref[...]) * (1.0 - lambda_init)
        o_ref[0, n:n + blk, :] = yt.T.astype(o_ref.dtype)

    col_max = {}
    for _ in scores(n_blocks - 1, col_max):
        pass
    for c in reversed(range(n_blocks)):
        gens = [outputs(c, col_max[c])]
        if c > 0:
            gens.insert(0, scores(c - 1, col_max))
        while gens:
            for g in list(gens):
                for _ in range(INTERLEAVE_GROUP):
                    if next(g, StopIteration) is StopIteration:
                        gens.remove(g)
                        break


def _prompt_attn(qb, kb, vtb, lams, subln_gcol, *, lambda_init, blk):
    B, T, _ = qb.shape
    grid = (N_HEADS, B)
    vec = lambda n: pl.BlockSpec((1, n), lambda h, b: (0, 0))
    qk_spec = pl.BlockSpec((1, T, HEAD_LANES), lambda h, b: (b, 0, h))
    vt_spec = pl.BlockSpec((1, VT_ROWS, T), lambda h, b: (b, h, 0))
    return pl.pallas_call(
        functools.partial(_prompt_attn_kernel, lambda_init=lambda_init, blk=blk),
        grid=grid,
        in_specs=[vec(HEAD_DIM)] * 4 + [pl.BlockSpec((V_DIM, 1), lambda h, b: (0, 0)), qk_spec, qk_spec, vt_spec],
        out_specs=qk_spec,
        out_shape=jax.ShapeDtypeStruct((B, T, V_WIDTH), BF16),
        scratch_shapes=[pltpu.VMEM((T - blk, HEAD_LANES), BF16), pltpu.VMEM((blk, 2 * blk), F32),
                        pltpu.VMEM((T, 2 * blk), F32), pltpu.VMEM((T, 2 * blk), F32)],
        compiler_params=pltpu.CompilerParams(dimension_semantics=("arbitrary",) * 2,
                                             vmem_limit_bytes=VMEM_LIMIT),
        name="prompt_attn",
    )(*lams, subln_gcol, qb, kb, vtb)


def _sample_attn_step(j, nj, lq1, lk1, lq2, lk2, g_ref, q_ref, kn_ref, vn_ref, ckt_ref, cv_ref, o_ref,
                      qrows_ref, m_ref, l_ref, acc_ref, *, lambda_init, past_len):
    tq = q_ref.shape[1]
    head_rows = 2 * tq
    n_rows = N_HEADS * head_rows
    kb = ckt_ref.shape[2]

    tq_bits = tq.bit_length() - 1
    assert tq == 1 << tq_bits
    row = lax.broadcasted_iota(jnp.int32, (n_rows, 1), 0)
    q_rel = row & (tq - 1)
    slope = _head_slope(row >> (tq_bits + 1))

    @pl.when(j == 0)
    def _():
        q = q_ref[0]
        tiled = jnp.concatenate([q] * (2 * N_HEADS), axis=0)
        r = lax.broadcasted_iota(jnp.int32, tiled.shape, 0)
        c = lax.broadcasted_iota(jnp.int32, tiled.shape, 1)
        qrows_ref[...] = jnp.where((r >> tq_bits) == (c >> HEAD_DIM_BITS), tiled, jnp.zeros_like(tiled))
        m_ref[...] = jnp.full(m_ref.shape, -jnp.inf, F32)
        l_ref[...] = jnp.zeros(l_ref.shape, F32)
        acc_ref[...] = jnp.zeros(acc_ref.shape, F32)

    def update(s, values):
        m_old = m_ref[...]
        m_new = jnp.maximum(m_old, jnp.max(s, axis=-1, keepdims=True))
        p = jnp.exp2(s - m_new)
        alpha = jnp.exp2(m_old - m_new)
        l_ref[...] = alpha * l_ref[...] + jnp.sum(p, axis=-1, keepdims=True)
        m_ref[...] = m_new
        acc_ref[...] = alpha * acc_ref[...] + _dot(p.astype(BF16), values)

    scores = []

    def score_cache():
        k_rel = (j * kb - past_len + lax.broadcasted_iota(jnp.int32, (1, kb), 1)).astype(F32)
        scores.append(_dot(qrows_ref[...], ckt_ref[0].astype(BF16)) + slope * k_rel)

    def attend_cache():
        values = jnp.concatenate(
            [cv_ref[0, pl.ds(h, kb, stride=N_HEADS), :].astype(BF16) for h in range(N_HEADS)], axis=1)
        update(scores.pop(), values)

    def finish_stream():
        @pl.when(j == nj - 1)
        def _():
            k_new = lax.broadcasted_iota(jnp.int32, (1, tq), 1)
            update(_nt_dot(qrows_ref[...], kn_ref[0]) + slope * (q_rel - jnp.abs(q_rel - k_new)).astype(F32),
                   vn_ref[0])
            lam = _lambda(lq1, lk1, lq2, lk2, lambda_init)
            out = acc_ref[...] / l_ref[...]
            for h in range(N_HEADS):
                own = out[h * head_rows:(h + 1) * head_rows, h * V_DIM:(h + 1) * V_DIM]
                o_ref[0, :, h * V_DIM:(h + 1) * V_DIM] = _sub_ln(own[:tq], own[tq:], lam, g_ref[...],
                                                                 lambda_init).astype(o_ref.dtype)

    return score_cache, attend_cache, finish_stream


def _out_mlp_kernel(pool_ref, attn_ref, x_ref, w_out_ref, gf_ref, w_up_ref, w_down_ref, gl_ref, y_ref):
    _out_mlp_tile(pool_ref, attn_ref, x_ref, w_out_ref, gf_ref, w_up_ref, w_down_ref, gl_ref, y_ref)


def _out_mlp_tile(pool_ref, attn_ref, x_ref, w_out_ref, gf_ref, w_up_ref, w_down_ref, gl_ref, y_ref,
                  after_chunk=()):
    mixed = (_dot(pool_ref[...], w_out_ref[0:POOL_WIDTH, :].astype(BF16))
             + _dot(attn_ref[...], w_out_ref[POOL_WIDTH:, :].astype(BF16)))
    h = x_ref[...] + mixed
    ms = jnp.mean(h * h, axis=-1, keepdims=True)
    hn = (h * lax.rsqrt(ms + EPS) * gf_ref[...]).astype(BF16)
    y = h
    for c in range(D_FF // FF_CHUNK):
        cols = slice(c * FF_CHUNK, (c + 1) * FF_CHUNK)
        a = jnp.maximum(_dot(hn, w_up_ref[:, cols]), 0.0)
        y = y + _dot((a * a).astype(BF16), w_down_ref[cols, :])
        if c < len(after_chunk):
            after_chunk[c]()
    ms = jnp.mean(y * y, axis=-1, keepdims=True)
    y_ref[...] = y * lax.rsqrt(ms + EPS) * gl_ref[...]


def _out_mlp(pool, attn, x, w_out, gf, w_up, w_down, gl, *, tm):
    rows = x.shape[0]
    row_spec = lambda width: pl.BlockSpec((tm, width), lambda r: (r, 0))
    const = lambda shape: pl.BlockSpec(shape, lambda r: (0, 0), pipeline_mode=pl.Buffered(1))
    return pl.pallas_call(
        _out_mlp_kernel,
        grid=(rows // tm,),
        in_specs=[row_spec(POOL_WIDTH), row_spec(V_WIDTH), row_spec(D_MODEL),
                  const((D_MODEL, D_MODEL)), const((1, D_MODEL)), const((D_MODEL, D_FF)),
                  const((D_FF, D_MODEL)), const((1, D_MODEL))],
        out_specs=row_spec(D_MODEL),
        out_shape=jax.ShapeDtypeStruct((rows, D_MODEL), F32),
        compiler_params=pltpu.CompilerParams(dimension_semantics=("arbitrary",),
                                             vmem_limit_bytes=VMEM_LIMIT),
        name="out_mlp",
    )(pool, attn, x, w_out, gf, w_up, w_down, gl)


def _out_mlp_sample_attn_kernel(*refs, parts, lambda_init, past_len):
    mlp_in, samp_in, (y_ref, o_ref), scratch = refs[:8], refs[8:18], refs[18:20], refs[20:]
    r = pl.program_id(0)
    score_cache, attend_cache, finish_stream = _sample_attn_step(
        r % parts, parts, *samp_in, o_ref, *scratch, lambda_init=lambda_init, past_len=past_len)
    _out_mlp_tile(*mlp_in, y_ref, after_chunk=(score_cache, attend_cache))
    finish_stream()


def _out_mlp_with_sample_attn(pool, attn, x, w_out, gf, w_up, w_down, gl,
                              qb, kb_new, vb_new, cache_kt, cache_v, lams, subln_g, *, tm, lambda_init):
    rows = x.shape[0]
    S, tq, _ = qb.shape
    past_len = cache_kt.shape[2]
    steps = rows // tm
    assert steps % S == 0, "every stream gets the same number of grid steps"
    parts = steps // S
    kblk = past_len // parts
    n_rows = 2 * N_HEADS * tq
    row_spec = lambda width: pl.BlockSpec((tm, width), lambda r: (r, 0))
    const = lambda shape: pl.BlockSpec(shape, lambda r: (0, 0), pipeline_mode=pl.Buffered(1))
    vec = lambda n: pl.BlockSpec((1, n), lambda r: (0, 0))
    new_spec = pl.BlockSpec((1, tq, QK_WIDTH), lambda r: (r // parts, 0, 0))
    kt_spec = pl.BlockSpec((1, QK_WIDTH, kblk), lambda r: (r // parts, 0, r % parts))
    v_spec = pl.BlockSpec((1, kblk * N_HEADS, V_DIM), lambda r: (r // parts, r % parts, 0))
    return pl.pallas_call(
        functools.partial(_out_mlp_sample_attn_kernel, parts=parts, lambda_init=lambda_init, past_len=past_len),
        grid=(steps,),
        in_specs=[row_spec(POOL_WIDTH), row_spec(V_WIDTH), row_spec(D_MODEL),
                  const((D_MODEL, D_MODEL)), const((1, D_MODEL)), const((D_MODEL, D_FF)),
                  const((D_FF, D_MODEL)), const((1, D_MODEL))]
                 + [vec(HEAD_DIM)] * 4 + [vec(V_DIM), new_spec, new_spec, new_spec, kt_spec, v_spec],
        out_specs=[row_spec(D_MODEL), new_spec],
        out_shape=[jax.ShapeDtypeStruct((rows, D_MODEL), F32), jax.ShapeDtypeStruct((S, tq, V_WIDTH), BF16)],
        scratch_shapes=[pltpu.VMEM((n_rows, QK_WIDTH), BF16), pltpu.VMEM((n_rows, 1), F32),
                        pltpu.VMEM((n_rows, 1), F32), pltpu.VMEM((n_rows, V_WIDTH), F32)],
        compiler_params=pltpu.CompilerParams(dimension_semantics=("arbitrary",),
                                             vmem_limit_bytes=VMEM_LIMIT),
        name="out_mlp_sample_attn",
    )(pool, attn, x, w_out, gf, w_up, w_down, gl, *lams, subln_g, qb, kb_new, vb_new, cache_kt, cache_v)


def _keys_from_feature_major(kt, frames):
    streams = kt.shape[0]
    return jnp.transpose(kt.reshape(streams, N_HEADS, 2, HEAD_DIM, frames), (0, 4, 1, 2, 3))[None]


def kernel(x_prompt, x_sample, state_pool, cache_k, cache_v, norm_mix_g, w_in, w_pool, pool_scale,
           lambda_q1, lambda_k1, lambda_q2, lambda_k2, subln_g, w_out, norm_ffn_g, w_up, w_down,
           norm_final_g):
    assert w_in.shape[0] == 1, "one layer per call"
    B, T, _ = x_prompt.shape
    S, TS, _ = x_sample.shape
    past_len = cache_k.shape[2]
    lambda_init = 0.8 - 0.6 * math.exp(-0.3 * 0)

    g_mix = norm_mix_g[0][None]
    g_ffn = norm_ffn_g[0][None]
    g_fin = norm_final_g[None]
    g_sub = subln_g[0][None]
    pscale = pool_scale[0][None]
    lams = (lambda_q1[0][None], lambda_k1[0][None], lambda_q2[0][None], lambda_k2[0][None])
    w_in_f, w_pool_f, w_out_f = w_in[0], w_pool[0], w_out[0]
    w_up_b = w_up[0].astype(BF16)
    w_down_b = w_down[0].astype(BF16)

    zero_buf = jnp.zeros((B, TAIL, POOL_WIDTH), F32)
    kt_p, v_p, qb, kb, vtb, pool_p, tail_p = _proj_pool(
        x_prompt, zero_buf, g_mix, w_in_f, w_pool_f, pscale, pos0=0, bb=1, tm=PROMPT_ROWS,
        feature_major=True)
    attn_p = _prompt_attn(qb, kb, vtb, lams, g_sub.reshape(V_DIM, 1), lambda_init=lambda_init, blk=ATTN_BLOCK)

    buf_s = jnp.pad(state_pool[0], ((0, 0), (TAIL - POOL_STATE, 0), (0, 0)))
    k_s, v_s, qb_s, kb_s, vb_s, pool_s, tail_s = _proj_pool(
        x_sample, buf_s, g_mix, w_in_f, w_pool_f, pscale, pos0=past_len, bb=S, tm=TS,
        feature_major=False)
    cache_kt = jnp.transpose(cache_k[0], (0, 2, 3, 4, 1)).reshape(S, QK_WIDTH, past_len)
    cache_vi = cache_v[0].reshape(S, past_len * N_HEADS, V_DIM)

    y_p, attn_s = _out_mlp_with_sample_attn(
        pool_p.reshape(B * T, -1), attn_p.reshape(B * T, -1), x_prompt.reshape(B * T, -1),
        w_out_f, g_ffn, w_up_b, w_down_b, g_fin, qb_s, kb_s, vb_s, cache_kt, cache_vi, lams, g_sub,
        tm=PROMPT_ROWS, lambda_init=lambda_init)
    y_p = y_p.reshape(B, T, D_MODEL)
    y_s = _out_mlp(pool_s.reshape(S * TS, -1), attn_s.reshape(S * TS, -1), x_sample.reshape(S * TS, -1),
                   w_out_f, g_ffn, w_up_b, w_down_b, g_fin, tm=S * TS).reshape(S, TS, D_MODEL)

    return (y_p, y_s,
            tail_p[:, TAIL - POOL_STATE:][None],
            _keys_from_feature_major(kt_p, T),
            v_p.reshape(1, B, T, N_HEADS, V_DIM),
            tail_s[:, TAIL - POOL_STATE:][None],
            k_s.reshape(1, S, TS, N_HEADS, 2, HEAD_DIM),
            v_s.reshape(1, S, TS, N_HEADS, V_DIM))
```

```python
import functools
import math

import jax
import jax.numpy as jnp
from jax import lax
from jax.experimental import pallas as pl
from jax.experimental.pallas import tpu as pltpu

D_MODEL = 1024
POOL_WIDTH = 512
POOL_WINDOWS = (2, 4, 8, 16)
POOL_GROUP = 128
POOL_STATE = 15
N_HEADS = 4
HEAD_DIM = 64
V_DIM = 128
QK_WIDTH = 512
V_WIDTH = N_HEADS * V_DIM
D_FF = 4096
CHUNK_BITS = 6
HEAD_DIM_BITS = 6
assert HEAD_DIM == 1 << HEAD_DIM_BITS
EPS = 1e-6
SUBLN_EPS = 1e-5

TAIL = 16
HEAD_LANES = 2 * HEAD_DIM
SUM_ROWS = 16
VT_ROWS = V_DIM + SUM_ROWS
BIAS_TERMS = 3
BIAS_ROWS = 16
assert 3 * BIAS_TERMS <= BIAS_ROWS
POS_LOW_BITS = 8

BF16 = jnp.bfloat16
F32 = jnp.float32
LOG2_E = math.log2(math.e)

PROJ_ROWS = 1024
PROMPT_ROWS = 512
ATTN_BLOCK = 256
FF_CHUNK = 1024
INTERLEAVE_GROUP = 1
VMEM_LIMIT = 56 * 1024 * 1024


def _nt_dot(a, b):
    return lax.dot_general(a, b, (((1,), (1,)), ((), ())), preferred_element_type=F32)


def _dot(a, b):
    return jnp.dot(a, b, preferred_element_type=F32)


def _head_slope(h):
    return lax.bitcast_convert_type((127 - 2 * (h + 1)) << 23, F32) * LOG2_E


def _lambda(lq1, lk1, lq2, lk2, lambda_init):
    return (jnp.exp(jnp.sum(lq1[...] * lk1[...], axis=-1, keepdims=True))
            - jnp.exp(jnp.sum(lq2[...] * lk2[...], axis=-1, keepdims=True)) + lambda_init)


def _sub_ln(o0, o1, lam, g, lambda_init):
    o = o0 - lam * o1
    ms = jnp.mean(o * o, axis=-1, keepdims=True)
    return (o * lax.rsqrt(ms + SUBLN_EPS) * g) * (1.0 - lambda_init)


def _proj_pool_kernel(x_ref, g_ref, w_in_ref, w_pool_ref, pscale_ref, buf_ref,
                      k_ref, v_ref, qb_ref, kb_ref, vb_ref, pool_ref, tail_ref, ext_ref,
                      *, pos0, feature_major):
    bb, tm, _ = x_ref.shape
    rows = bb * tm
    t = pl.program_id(1)

    @pl.when(t == 0)
    def _():
        ext_ref[:, 0:TAIL, :] = buf_ref[...]

    x = x_ref[...].reshape(rows, D_MODEL)
    ms = jnp.mean(x * x, axis=-1, keepdims=True)
    hn = (x * lax.rsqrt(ms + EPS) * g_ref[...]).astype(BF16)

    def proj(j):
        return _dot(hn, w_in_ref[:, j * 512:(j + 1) * 512].astype(BF16))

    ext_ref[:, TAIL:TAIL + tm, :] = proj(0).reshape(bb, tm, POOL_WIDTH)
    pos = pos0 + t * tm + lax.broadcasted_iota(jnp.int32, (1, tm, 1), 1)

    def pool_group(g):
        w = POOL_WINDOWS[g]
        lanes = slice(g * POOL_GROUP, (g + 1) * POOL_GROUP)
        ext = ext_ref[:, :, lanes]
        acc, span = ext, 1
        while span < w:
            acc = acc + pltpu.roll(acc, span, axis=1)
            span *= 2
        cnt = jnp.minimum(pos + 1, w).astype(F32)
        diff = (acc[:, TAIL:, :] / cnt - ext[:, TAIL:, :]).reshape(rows, POOL_GROUP).astype(BF16)
        y = _dot(diff, w_pool_ref[g].astype(BF16)) * pscale_ref[:, lanes]
        pool_ref[:, :, lanes] = y.astype(BF16).reshape(bb, tm, POOL_GROUP)

    k = proj(2)
    kb_ref[...] = k.astype(BF16).reshape(bb, tm, QK_WIDTH)
    if feature_major:
        assert bb == 1
        k_ref[0] = k.T
    else:
        k_ref[...] = k.reshape(bb, tm, QK_WIDTH)
    pool_group(0)
    pool_group(1)
    v = proj(3)
    if feature_major:
        vt = v.T.astype(BF16)
        for h in range(N_HEADS):
            vb_ref[0, h * VT_ROWS:h * VT_ROWS + V_DIM, :] = vt[h * V_DIM:(h + 1) * V_DIM]
            vb_ref[0, h * VT_ROWS + V_DIM:(h + 1) * VT_ROWS, :] = jnp.ones((SUM_ROWS, tm), BF16)
    else:
        vb_ref[...] = v.astype(BF16).reshape(bb, tm, V_WIDTH)
    for h in range(N_HEADS):
        v_ref[:, pl.ds(h, tm, stride=N_HEADS), :] = v[:, h * V_DIM:(h + 1) * V_DIM].reshape(bb, tm, V_DIM)
    pool_group(2)
    pool_group(3)
    q = proj(1)
    qb_ref[...] = (q * (LOG2_E / math.sqrt(HEAD_DIM))).astype(BF16).reshape(bb, tm, QK_WIDTH)

    tail = ext_ref[:, tm:tm + TAIL, :]
    tail_ref[...] = tail
    ext_ref[:, 0:TAIL, :] = tail


def _proj_pool(x, buf, g, w_in, w_pool, pscale, *, pos0, bb, tm, feature_major):
    B, T, _ = x.shape
    grid = (B // bb, T // tm)
    row_spec = lambda width: pl.BlockSpec((bb, tm, width), lambda b, t: (b, t, 0))
    const = lambda shape: pl.BlockSpec(shape, lambda b, t: (0,) * len(shape), pipeline_mode=pl.Buffered(1))
    seq_spec = pl.BlockSpec((bb, TAIL, POOL_WIDTH), lambda b, t: (b, 0, 0))
    act = lambda dtype: jax.ShapeDtypeStruct((B, T, QK_WIDTH), dtype)
    if feature_major:
        fm_spec = lambda rows: pl.BlockSpec((bb, rows, tm), lambda b, t: (b, 0, t))
        fm_shape = lambda rows, dtype: jax.ShapeDtypeStruct((B, rows, T), dtype)
        k_spec, k_shape = fm_spec(QK_WIDTH), fm_shape(QK_WIDTH, F32)
        vb_spec, vb_shape = fm_spec(N_HEADS * VT_ROWS), fm_shape(N_HEADS * VT_ROWS, BF16)
    else:
        k_spec, k_shape = row_spec(QK_WIDTH), act(F32)
        vb_spec, vb_shape = row_spec(V_WIDTH), act(BF16)
    return pl.pallas_call(
        functools.partial(_proj_pool_kernel, pos0=pos0, feature_major=feature_major),
        grid=grid,
        in_specs=[row_spec(D_MODEL), const((1, D_MODEL)), const((D_MODEL, 4 * 512)),
                  const((len(POOL_WINDOWS), POOL_GROUP, POOL_GROUP)), const((1, POOL_WIDTH)), seq_spec],
        out_specs=[k_spec, pl.BlockSpec((bb, tm * N_HEADS, V_DIM), lambda b, t: (b, t, 0)),
                   row_spec(QK_WIDTH), row_spec(QK_WIDTH), vb_spec, row_spec(POOL_WIDTH), seq_spec],
        out_shape=[k_shape, jax.ShapeDtypeStruct((B, T * N_HEADS, V_DIM), F32),
                   act(BF16), act(BF16), vb_shape, act(BF16),
                   jax.ShapeDtypeStruct((B, TAIL, POOL_WIDTH), F32)],
        scratch_shapes=[pltpu.VMEM((bb, TAIL + tm, POOL_WIDTH), F32)],
        compiler_params=pltpu.CompilerParams(dimension_semantics=("arbitrary", "arbitrary"),
                                             vmem_limit_bytes=VMEM_LIMIT),
        name="proj_pool",
    )(x, g, w_in, w_pool, pscale, buf)


def _prompt_attn_kernel(lq1, lk1, lq2, lk2, gcol_ref, q_ref, k_ref, vt_ref, o_ref,
                        key_pos_ref, bias_diag_ref, s_even_ref, s_odd_ref, *, lambda_init, blk):
    n_blocks = q_ref.shape[1] // blk
    h = pl.program_id(0)
    slope = _head_slope(jnp.full((1, 1), h, jnp.int32))

    @pl.when((h == 0) & (pl.program_id(1) == 0))
    def _():
        kg = lax.broadcasted_iota(jnp.int32, key_pos_ref.shape, 0)
        lane = lax.broadcasted_iota(jnp.int32, key_pos_ref.shape, 1)
        code = jnp.where(lane < BIAS_TERMS, kg >> POS_LOW_BITS,
                         jnp.where(lane < 2 * BIAS_TERMS, kg & ((1 << POS_LOW_BITS) - 1),
                                   jnp.where(lane < 3 * BIAS_TERMS, 1, 0)))
        key_pos_ref[...] = code.astype(F32).astype(BF16)

    @pl.when(pl.program_id(1) == 0)
    def _():
        kk = lax.broadcasted_iota(jnp.int32, bias_diag_ref.shape, 0)
        qq = lax.broadcasted_iota(jnp.int32, bias_diag_ref.shape, 1) & (blk - 1)
        visible = (kk >> CHUNK_BITS) <= (qq >> CHUNK_BITS)
        bias_diag_ref[...] = jnp.where(visible, -slope * jnp.abs(qq - kk).astype(F32), -jnp.inf)

    lam = _lambda(lq1, lk1, lq2, lk2, lambda_init)
    feat = lax.broadcasted_iota(jnp.int32, (HEAD_LANES, blk), 0)
    zero = jnp.zeros((HEAD_LANES, blk), BF16)
    q_off = (lax.broadcasted_iota(jnp.int32, (1, 2 * blk), 1) & (blk - 1)).astype(F32)
    bias_row = lax.broadcasted_iota(jnp.int32, (BIAS_ROWS, 2 * blk), 0)

    def pieces(x):
        out = []
        for _ in range(BIAS_TERMS):
            piece = x.astype(BF16).astype(F32)
            out.append(piece)
            x = x - piece
        return out

    s_slot = (s_even_ref, s_odd_ref)
    run_time_zero = jnp.minimum(pl.program_id(1), 0)

    def sublane_max(s):
        return jnp.max(s.reshape(blk // 8, 8, s.shape[1]), axis=0)

    def scores(c, col_max):
        n = c * blk
        qt = q_ref[0, n:n + blk, :].astype(F32).T.astype(BF16)
        q_cols = jnp.concatenate([jnp.where(feat < HEAD_DIM, qt, zero), jnp.where(feat >= HEAD_DIM, qt, zero)],
                                 axis=1)
        s_diag = _dot(k_ref[0, n:n + blk, :], q_cols) + bias_diag_ref[...]
        s_slot[c % 2][n:n + blk, :] = s_diag
        m = sublane_max(s_diag)
        yield
        if c > 0:
            terms = (pieces(slope * float(1 << POS_LOW_BITS)) + pieces(slope)
                     + pieces(-slope * (q_off + float(n))))
            bias_cols = jnp.zeros(bias_row.shape, F32)
            for r, term in enumerate(terms):
                bias_cols = jnp.where(bias_row == r, term, bias_cols)
            w = jnp.concatenate([q_cols, bias_cols.astype(BF16),
                                 jnp.zeros((HEAD_LANES - BIAS_ROWS, 2 * blk), BF16)], axis=0)
            for j in range(c):
                rows = slice(j * blk, (j + 1) * blk)
                s_past = _dot(jnp.concatenate([k_ref[0, rows, :], key_pos_ref[rows, :]], axis=1), w)
                s_slot[c % 2][rows, :] = s_past
                m = jnp.maximum(m, sublane_max(s_past))
                yield
        col_max[c] = jnp.max(m, axis=0, keepdims=True)

    def outputs(c, m):
        n = c * blk
        acc = None
        for j in range(c + 1):
            rows = slice(j * blk, (j + 1) * blk)
            s = s_slot[c % 2][pl.ds(pl.multiple_of(run_time_zero + j * blk, blk), blk), :]
            pv = _dot(vt_ref[0, :, rows], jnp.exp2(s - m).astype(BF16))
            acc = pv if acc is None else acc + pv
            yield
        o = [acc[0:V_DIM, half * blk:(half + 1) * blk] / acc[V_DIM:V_DIM + 1, half * blk:(half + 1) * blk]
             for half in range(2)]
        ot = o[0] - lam * o[1]
        ms = jnp.mean(ot * ot, axis=0, keepdims=True)
        yt = (ot * lax.rsqrt(ms + SUBLN_EPS) * gcol_ref[...]) * (1.0 - lambda_init)
        o_ref[0, n:n + blk, :] = yt.T.astype(o_ref.dtype)

    col_max = {}
    for _ in scores(n_blocks - 1, col_max):
        pass
    for c in reversed(range(n_blocks)):
        gens = [outputs(c, col_max[c])]
        if c > 0:
            gens.insert(0, scores(c - 1, col_max))
        while gens:
            for g in list(gens):
                for _ in range(INTERLEAVE_GROUP):
                    if next(g, StopIteration) is StopIteration:
                        gens.remove(g)
                        break


def _prompt_attn(qb, kb, vtb, lams, subln_gcol, *, lambda_init, blk):
    B, T, _ = qb.shape
    grid = (N_HEADS, B)
    vec = lambda n: pl.BlockSpec((1, n), lambda h, b: (0, 0))
    qk_spec = pl.BlockSpec((1, T, HEAD_LANES), lambda h, b: (b, 0, h))
    vt_spec = pl.BlockSpec((1, VT_ROWS, T), lambda h, b: (b, h, 0))
    return pl.pallas_call(
        functools.partial(_prompt_attn_kernel, lambda_init=lambda_init, blk=blk),
        grid=grid,
        in_specs=[vec(HEAD_DIM)] * 4 + [pl.BlockSpec((V_DIM, 1), lambda h, b: (0, 0)), qk_spec, qk_spec, vt_spec],
        out_specs=qk_spec,
        out_shape=jax.ShapeDtypeStruct((B, T, V_WIDTH), BF16),
        scratch_shapes=[pltpu.VMEM((T - blk, HEAD_LANES), BF16), pltpu.VMEM((blk, 2 * blk), F32),
                        pltpu.VMEM((T, 2 * blk), F32), pltpu.VMEM((T, 2 * blk), F32)],
        compiler_params=pltpu.CompilerParams(dimension_semantics=("arbitrary",) * 2,
                                             vmem_limit_bytes=VMEM_LIMIT),
        name="prompt_attn",
    )(*lams, subln_gcol, qb, kb, vtb)


def _sample_attn_step(j, nj, lq1, lk1, lq2, lk2, g_ref, q_ref, kn_ref, vn_ref, ckt_ref, cv_ref, o_ref,
                      qrows_ref, m_ref, l_ref, acc_ref, *, lambda_init, past_len):
    tq = q_ref.shape[1]
    head_rows = 2 * tq
    n_rows = N_HEADS * head_rows
    kb = ckt_ref.shape[2]

    tq_bits = tq.bit_length() - 1
    assert tq == 1 << tq_bits
    row = lax.broadcasted_iota(jnp.int32, (n_rows, 1), 0)
    q_rel = row & (tq - 1)
    slope = _head_slope(row >> (tq_bits + 1))

    @pl.when(j == 0)
    def _():
        q = q_ref[0]
        tiled = jnp.concatenate([q] * (2 * N_HEADS), axis=0)
        r = lax.broadcasted_iota(jnp.int32, tiled.shape, 0)
        c = lax.broadcasted_iota(jnp.int32, tiled.shape, 1)
        qrows_ref[...] = jnp.where((r >> tq_bits) == (c >> HEAD_DIM_BITS), tiled, jnp.zeros_like(tiled))
        m_ref[...] = jnp.full(m_ref.shape, -jnp.inf, F32)
        l_ref[...] = jnp.zeros(l_ref.shape, F32)
        acc_ref[...] = jnp.zeros(acc_ref.shape, F32)

    def update(s, values):
        m_old = m_ref[...]
        m_new = jnp.maximum(m_old, jnp.max(s, axis=-1, keepdims=True))
        p = jnp.exp2(s - m_new)
        alpha = jnp.exp2(m_old - m_new)
        l_ref[...] = alpha * l_ref[...] + jnp.sum(p, axis=-1, keepdims=True)
        m_ref[...] = m_new
        acc_ref[...] = alpha * acc_ref[...] + _dot(p.astype(BF16), values)

    scores = []

    def score_cache():
        k_rel = (j * kb - past_len + lax.broadcasted_iota(jnp.int32, (1, kb), 1)).astype(F32)
        scores.append(_dot(qrows_ref[...], ckt_ref[0].astype(BF16)) + slope * k_rel)

    def attend_cache():
        values = jnp.concatenate(
            [cv_ref[0, pl.ds(h, kb, stride=N_HEADS), :].astype(BF16) for h in range(N_HEADS)], axis=1)
        update(scores.pop(), values)

    def finish_stream():
        @pl.when(j == nj - 1)
        def _():
            k_new = lax.broadcasted_iota(jnp.int32, (1, tq), 1)
            update(_nt_dot(qrows_ref[...], kn_ref[0]) + slope * (q_rel - jnp.abs(q_rel - k_new)).astype(F32),
                   vn_ref[0])
            lam = _lambda(lq1, lk1, lq2, lk2, lambda_init)
            out = acc_ref[...] / l_ref[...]
            for h in range(N_HEADS):
                own = out[h * head_rows:(h + 1) * head_rows, h * V_DIM:(h + 1) * V_DIM]
                o_ref[0, :, h * V_DIM:(h + 1) * V_DIM] = _sub_ln(own[:tq], own[tq:], lam, g_ref[...],
                                                                 lambda_init).astype(o_ref.dtype)

    return score_cache, attend_cache, finish_stream


def _out_mlp_kernel(pool_ref, attn_ref, x_ref, w_out_ref, gf_ref, w_up_ref, w_down_ref, gl_ref, y_ref):
    _out_mlp_tile(pool_ref, attn_ref, x_ref, w_out_ref, gf_ref, w_up_ref, w_down_ref, gl_ref, y_ref)


def _out_mlp_tile(pool_ref, attn_ref, x_ref, w_out_ref, gf_ref, w_up_ref, w_down_ref, gl_ref, y_ref,
                  after_chunk=()):
    mixed = (_dot(pool_ref[...], w_out_ref[0:POOL_WIDTH, :].astype(BF16))
             + _dot(attn_ref[...], w_out_ref[POOL_WIDTH:, :].astype(BF16)))
    h = x_ref[...] + mixed
    ms = jnp.mean(h * h, axis=-1, keepdims=True)
    hn = (h * lax.rsqrt(ms + EPS) * gf_ref[...]).astype(BF16)
    y = h
    for c in range(D_FF // FF_CHUNK):
        cols = slice(c * FF_CHUNK, (c + 1) * FF_CHUNK)
        a = jnp.maximum(_dot(hn, w_up_ref[:, cols]), 0.0)
        y = y + _dot((a * a).astype(BF16), w_down_ref[cols, :])
        if c < len(after_chunk):
            after_chunk[c]()
    ms = jnp.mean(y * y, axis=-1, keepdims=True)
    y_ref[...] = y * lax.rsqrt(ms + EPS) * gl_ref[...]


def _out_mlp(pool, attn, x, w_out, gf, w_up, w_down, gl, *, tm):
    rows = x.shape[0]
    row_spec = lambda width: pl.BlockSpec((tm, width), lambda r: (r, 0))
    const = lambda shape: pl.BlockSpec(shape, lambda r: (0, 0), pipeline_mode=pl.Buffered(1))
    return pl.pallas_call(
        _out_mlp_kernel,
        grid=(rows // tm,),
        in_specs=[row_spec(POOL_WIDTH), row_spec(V_WIDTH), row_spec(D_MODEL),
                  const((D_MODEL, D_MODEL)), const((1, D_MODEL)), const((D_MODEL, D_FF)),
                  const((D_FF, D_MODEL)), const((1, D_MODEL))],
        out_specs=row_spec(D_MODEL),
        out_shape=jax.ShapeDtypeStruct((rows, D_MODEL), F32),
        compiler_params=pltpu.CompilerParams(dimension_semantics=("arbitrary",),
                                             vmem_limit_bytes=VMEM_LIMIT),
        name="out_mlp",
    )(pool, attn, x, w_out, gf, w_up, w_down, gl)


def _out_mlp_sample_attn_kernel(*refs, parts, lambda_init, past_len):
    mlp_in, samp_in, (y_ref, o_ref), scratch = refs[:8], refs[8:18], refs[18:20], refs[20:]
    r = pl.program_id(0)
    score_cache, attend_cache, finish_stream = _sample_attn_step(
        r % parts, parts, *samp_in, o_ref, *scratch, lambda_init=lambda_init, past_len=past_len)
    _out_mlp_tile(*mlp_in, y_ref, after_chunk=(score_cache, attend_cache))
    finish_stream()


def _out_mlp_with_sample_attn(pool, attn, x, w_out, gf, w_up, w_down, gl,
                              qb, kb_new, vb_new, cache_kt, cache_v, lams, subln_g, *, tm, lambda_init):
    rows = x.shape[0]
    S, tq, _ = qb.shape
    past_len = cache_kt.shape[2]
    steps = rows // tm
    assert steps % S == 0, "every stream gets the same number of grid steps"
    parts = steps // S
    kblk = past_len // parts
    n_rows = 2 * N_HEADS * tq
    row_spec = lambda width: pl.BlockSpec((tm, width), lambda r: (r, 0))
    const = lambda shape: pl.BlockSpec(shape, lambda r: (0, 0), pipeline_mode=pl.Buffered(1))
    vec = lambda n: pl.BlockSpec((1, n), lambda r: (0, 0))
    new_spec = pl.BlockSpec((1, tq, QK_WIDTH), lambda r: (r // parts, 0, 0))
    kt_spec = pl.BlockSpec((1, QK_WIDTH, kblk), lambda r: (r // parts, 0, r % parts))
    v_spec = pl.BlockSpec((1, kblk * N_HEADS, V_DIM), lambda r: (r // parts, r % parts, 0))
    return pl.pallas_call(
        functools.partial(_out_mlp_sample_attn_kernel, parts=parts, lambda_init=lambda_init, past_len=past_len),
        grid=(steps,),
        in_specs=[row_spec(POOL_WIDTH), row_spec(V_WIDTH), row_spec(D_MODEL),
                  const((D_MODEL, D_MODEL)), const((1, D_MODEL)), const((D_MODEL, D_FF)),
                  const((D_FF, D_MODEL)), const((1, D_MODEL))]
                 + [vec(HEAD_DIM)] * 4 + [vec(V_DIM), new_spec, new_spec, new_spec, kt_spec, v_spec],
        out_specs=[row_spec(D_MODEL), new_spec],
        out_shape=[jax.ShapeDtypeStruct((rows, D_MODEL), F32), jax.ShapeDtypeStruct((S, tq, V_WIDTH), BF16)],
        scratch_shapes=[pltpu.VMEM((n_rows, QK_WIDTH), BF16), pltpu.VMEM((n_rows, 1), F32),
                        pltpu.VMEM((n_rows, 1), F32), pltpu.VMEM((n_rows, V_WIDTH), F32)],
        compiler_params=pltpu.CompilerParams(dimension_semantics=("arbitrary",),
                                             vmem_limit_bytes=VMEM_LIMIT),
        name="out_mlp_sample_attn",
    )(pool, attn, x, w_out, gf, w_up, w_down, gl, *lams, subln_g, qb, kb_new, vb_new, cache_kt, cache_v)


def _keys_from_feature_major(kt, frames):
    streams = kt.shape[0]
    return jnp.transpose(kt.reshape(streams, N_HEADS, 2, HEAD_DIM, frames), (0, 4, 1, 2, 3))[None]


def kernel(x_prompt, x_sample, state_pool, cache_k, cache_v, norm_mix_g, w_in, w_pool, pool_scale,
           lambda_q1, lambda_k1, lambda_q2, lambda_k2, subln_g, w_out, norm_ffn_g, w_up, w_down,
           norm_final_g):
    assert w_in.shape[0] == 1, "one layer per call"
    B, T, _ = x_prompt.shape
    S, TS, _ = x_sample.shape
    past_len = cache_k.shape[2]
    lambda_init = 0.8 - 0.6 * math.exp(-0.3 * 0)

    g_mix = norm_mix_g[0][None]
    g_ffn = norm_ffn_g[0][None]
    g_fin = norm_final_g[None]
    g_sub = subln_g[0][None]
    pscale = pool_scale[0][None]
    lams = (lambda_q1[0][None], lambda_k1[0][None], lambda_q2[0][None], lambda_k2[0][None])
    w_in_f, w_pool_f, w_out_f = w_in[0], w_pool[0], w_out[0]
    w_up_b = w_up[0].astype(BF16)
    w_down_b = w_down[0].astype(BF16)

    zero_buf = jnp.zeros((B, TAIL, POOL_WIDTH), F32)
    kt_p, v_p, qb, kb, vtb, pool_p, tail_p = _proj_pool(
        x_prompt, zero_buf, g_mix, w_in_f, w_pool_f, pscale, pos0=0, bb=1, tm=PROJ_ROWS,
        feature_major=True)
    attn_p = _prompt_attn(qb, kb, vtb, lams, g_sub.reshape(V_DIM, 1), lambda_init=lambda_init, blk=ATTN_BLOCK)

    buf_s = jnp.pad(state_pool[0], ((0, 0), (TAIL - POOL_STATE, 0), (0, 0)))
    k_s, v_s, qb_s, kb_s, vb_s, pool_s, tail_s = _proj_pool(
        x_sample, buf_s, g_mix, w_in_f, w_pool_f, pscale, pos0=past_len, bb=S, tm=TS,
        feature_major=False)
    cache_kt = jnp.transpose(cache_k[0], (0, 2, 3, 4, 1)).reshape(S, QK_WIDTH, past_len)
    cache_vi = cache_v[0].reshape(S, past_len * N_HEADS, V_DIM)

    y_p, attn_s = _out_mlp_with_sample_attn(
        pool_p.reshape(B * T, -1), attn_p.reshape(B * T, -1), x_prompt.reshape(B * T, -1),
        w_out_f, g_ffn, w_up_b, w_down_b, g_fin, qb_s, kb_s, vb_s, cache_kt, cache_vi, lams, g_sub,
        tm=PROMPT_ROWS, lambda_init=lambda_init)
    y_p = y_p.reshape(B, T, D_MODEL)
    y_s = _out_mlp(pool_s.reshape(S * TS, -1), attn_s.reshape(S * TS, -1), x_sample.reshape(S * TS, -1),
                   w_out_f, g_ffn, w_up_b, w_down_b, g_fin, tm=S * TS).reshape(S, TS, D_MODEL)

    return (y_p, y_s,
            tail_p[:, TAIL - POOL_STATE:][None],
            _keys_from_feature_major(kt_p, T),
            v_p.reshape(1, B, T, N_HEADS, V_DIM),
            tail_s[:, TAIL - POOL_STATE:][None],
            k_s.reshape(1, S, TS, N_HEADS, 2, HEAD_DIM),
            v_s.reshape(1, S, TS, N_HEADS, V_DIM))
```

```python
import functools
import math

import jax
import jax.numpy as jnp
from jax import lax
from jax.experimental import pallas as pl
from jax.experimental.pallas import tpu as pltpu

D_MODEL = 1024
POOL_WIDTH = 512
POOL_WINDOWS = (2, 4, 8, 16)
POOL_GROUP = 128
POOL_STATE = 15
N_HEADS = 4
HEAD_DIM = 64
V_DIM = 128
QK_WIDTH = 512
V_WIDTH = N_HEADS * V_DIM
D_FF = 4096
CHUNK_BITS = 6
HEAD_DIM_BITS = 6
assert HEAD_DIM == 1 << HEAD_DIM_BITS
EPS = 1e-6
SUBLN_EPS = 1e-5

TAIL = 16
HEAD_LANES = 2 * HEAD_DIM
SUM_ROWS = 16
VT_ROWS = V_DIM + SUM_ROWS
BIAS_TERMS = 3
BIAS_ROWS = 16
assert 3 * BIAS_TERMS <= BIAS_ROWS
POS_LOW_BITS = 8

BF16 = jnp.bfloat16
F32 = jnp.float32
LOG2_E = math.log2(math.e)

PROJ_ROWS = 1024
PROMPT_ROWS = 512
ATTN_BLOCK = 256
FF_CHUNK = 1024
INTERLEAVE_GROUP = 1
SCORE_SLOTS = 3
VMEM_LIMIT = 56 * 1024 * 1024


def _nt_dot(a, b):
    return lax.dot_general(a, b, (((1,), (1,)), ((), ())), preferred_element_type=F32)


def _dot(a, b):
    return jnp.dot(a, b, preferred_element_type=F32)


def _head_slope(h):
    return lax.bitcast_convert_type((127 - 2 * (h + 1)) << 23, F32) * LOG2_E


def _lambda(lq1, lk1, lq2, lk2, lambda_init):
    return (jnp.exp(jnp.sum(lq1[...] * lk1[...], axis=-1, keepdims=True))
            - jnp.exp(jnp.sum(lq2[...] * lk2[...], axis=-1, keepdims=True)) + lambda_init)


def _sub_ln(o0, o1, lam, g, lambda_init):
    o = o0 - lam * o1
    ms = jnp.mean(o * o, axis=-1, keepdims=True)
    return (o * lax.rsqrt(ms + SUBLN_EPS) * g) * (1.0 - lambda_init)


def _proj_pool_kernel(x_ref, g_ref, w_in_ref, w_pool_ref, pscale_ref, buf_ref,
                      k_ref, v_ref, qb_ref, kb_ref, vb_ref, pool_ref, tail_ref, ext_ref,
                      *, pos0, feature_major):
    bb, tm, _ = x_ref.shape
    rows = bb * tm
    t = pl.program_id(1)

    @pl.when(t == 0)
    def _():
        ext_ref[:, 0:TAIL, :] = buf_ref[...]

    x = x_ref[...].reshape(rows, D_MODEL)
    ms = jnp.mean(x * x, axis=-1, keepdims=True)
    hn = (x * lax.rsqrt(ms + EPS) * g_ref[...]).astype(BF16)

    def proj(j):
        return _dot(hn, w_in_ref[:, j * 512:(j + 1) * 512].astype(BF16))

    ext_ref[:, TAIL:TAIL + tm, :] = proj(0).reshape(bb, tm, POOL_WIDTH)
    pos = pos0 + t * tm + lax.broadcasted_iota(jnp.int32, (1, tm, 1), 1)

    def pool_group(g):
        w = POOL_WINDOWS[g]
        lanes = slice(g * POOL_GROUP, (g + 1) * POOL_GROUP)
        ext = ext_ref[:, :, lanes]
        acc, span = ext, 1
        while span < w:
            acc = acc + pltpu.roll(acc, span, axis=1)
            span *= 2
        cnt = jnp.minimum(pos + 1, w).astype(F32)
        diff = (acc[:, TAIL:, :] / cnt - ext[:, TAIL:, :]).reshape(rows, POOL_GROUP).astype(BF16)
        y = _dot(diff, w_pool_ref[g].astype(BF16)) * pscale_ref[:, lanes]
        pool_ref[:, :, lanes] = y.astype(BF16).reshape(bb, tm, POOL_GROUP)

    k = proj(2)
    kb_ref[...] = k.astype(BF16).reshape(bb, tm, QK_WIDTH)
    if feature_major:
        assert bb == 1
        k_ref[0] = k.T
    else:
        k_ref[...] = k.reshape(bb, tm, QK_WIDTH)
    pool_group(0)
    pool_group(1)
    v = proj(3)
    if feature_major:
        vt = v.T.astype(BF16)
        for h in range(N_HEADS):
            vb_ref[0, h * VT_ROWS:h * VT_ROWS + V_DIM, :] = vt[h * V_DIM:(h + 1) * V_DIM]
            vb_ref[0, h * VT_ROWS + V_DIM:(h + 1) * VT_ROWS, :] = jnp.ones((SUM_ROWS, tm), BF16)
    else:
        vb_ref[...] = v.astype(BF16).reshape(bb, tm, V_WIDTH)
    for h in range(N_HEADS):
        v_ref[:, pl.ds(h, tm, stride=N_HEADS), :] = v[:, h * V_DIM:(h + 1) * V_DIM].reshape(bb, tm, V_DIM)
    pool_group(2)
    pool_group(3)
    q = proj(1)
    qb_ref[...] = (q * (LOG2_E / math.sqrt(HEAD_DIM))).astype(BF16).reshape(bb, tm, QK_WIDTH)

    tail = ext_ref[:, tm:tm + TAIL, :]
    tail_ref[...] = tail
    ext_ref[:, 0:TAIL, :] = tail


def _proj_pool(x, buf, g, w_in, w_pool, pscale, *, pos0, bb, tm, feature_major):
    B, T, _ = x.shape
    grid = (B // bb, T // tm)
    row_spec = lambda width: pl.BlockSpec((bb, tm, width), lambda b, t: (b, t, 0))
    const = lambda shape: pl.BlockSpec(shape, lambda b, t: (0,) * len(shape), pipeline_mode=pl.Buffered(1))
    seq_spec = pl.BlockSpec((bb, TAIL, POOL_WIDTH), lambda b, t: (b, 0, 0))
    act = lambda dtype: jax.ShapeDtypeStruct((B, T, QK_WIDTH), dtype)
    if feature_major:
        fm_spec = lambda rows: pl.BlockSpec((bb, rows, tm), lambda b, t: (b, 0, t))
        fm_shape = lambda rows, dtype: jax.ShapeDtypeStruct((B, rows, T), dtype)
        k_spec, k_shape = fm_spec(QK_WIDTH), fm_shape(QK_WIDTH, F32)
        vb_spec, vb_shape = fm_spec(N_HEADS * VT_ROWS), fm_shape(N_HEADS * VT_ROWS, BF16)
    else:
        k_spec, k_shape = row_spec(QK_WIDTH), act(F32)
        vb_spec, vb_shape = row_spec(V_WIDTH), act(BF16)
    return pl.pallas_call(
        functools.partial(_proj_pool_kernel, pos0=pos0, feature_major=feature_major),
        grid=grid,
        in_specs=[row_spec(D_MODEL), const((1, D_MODEL)), const((D_MODEL, 4 * 512)),
                  const((len(POOL_WINDOWS), POOL_GROUP, POOL_GROUP)), const((1, POOL_WIDTH)), seq_spec],
        out_specs=[k_spec, pl.BlockSpec((bb, tm * N_HEADS, V_DIM), lambda b, t: (b, t, 0)),
                   row_spec(QK_WIDTH), row_spec(QK_WIDTH), vb_spec, row_spec(POOL_WIDTH), seq_spec],
        out_shape=[k_shape, jax.ShapeDtypeStruct((B, T * N_HEADS, V_DIM), F32),
                   act(BF16), act(BF16), vb_shape, act(BF16),
                   jax.ShapeDtypeStruct((B, TAIL, POOL_WIDTH), F32)],
        scratch_shapes=[pltpu.VMEM((bb, TAIL + tm, POOL_WIDTH), F32)],
        compiler_params=pltpu.CompilerParams(dimension_semantics=("arbitrary", "arbitrary"),
                                             vmem_limit_bytes=VMEM_LIMIT),
        name="proj_pool",
    )(x, g, w_in, w_pool, pscale, buf)


def _prompt_attn_kernel(lq1, lk1, lq2, lk2, gcol_ref, q_ref, k_ref, vt_ref, o_ref,
                        key_pos_ref, bias_diag_ref, *s_slot, lambda_init, blk):
    n_blocks = q_ref.shape[1] // blk
    h = pl.program_id(0)
    slope = _head_slope(jnp.full((1, 1), h, jnp.int32))

    @pl.when((h == 0) & (pl.program_id(1) == 0))
    def _():
        kg = lax.broadcasted_iota(jnp.int32, key_pos_ref.shape, 0)
        lane = lax.broadcasted_iota(jnp.int32, key_pos_ref.shape, 1)
        code = jnp.where(lane < BIAS_TERMS, kg >> POS_LOW_BITS,
                         jnp.where(lane < 2 * BIAS_TERMS, kg & ((1 << POS_LOW_BITS) - 1),
                                   jnp.where(lane < 3 * BIAS_TERMS, 1, 0)))
        key_pos_ref[...] = code.astype(F32).astype(BF16)

    @pl.when(pl.program_id(1) == 0)
    def _():
        kk = lax.broadcasted_iota(jnp.int32, bias_diag_ref.shape, 0)
        qq = lax.broadcasted_iota(jnp.int32, bias_diag_ref.shape, 1) & (blk - 1)
        visible = (kk >> CHUNK_BITS) <= (qq >> CHUNK_BITS)
        bias_diag_ref[...] = jnp.where(visible, -slope * jnp.abs(qq - kk).astype(F32), -jnp.inf)

    lam = _lambda(lq1, lk1, lq2, lk2, lambda_init)
    feat = lax.broadcasted_iota(jnp.int32, (HEAD_LANES, blk), 0)
    zero = jnp.zeros((HEAD_LANES, blk), BF16)
    q_off = (lax.broadcasted_iota(jnp.int32, (1, 2 * blk), 1) & (blk - 1)).astype(F32)
    bias_row = lax.broadcasted_iota(jnp.int32, (BIAS_ROWS, 2 * blk), 0)

    def pieces(x):
        out = []
        for _ in range(BIAS_TERMS):
            piece = x.astype(BF16).astype(F32)
            out.append(piece)
            x = x - piece
        return out

    n_slots = len(s_slot)
    run_time_zero = jnp.minimum(pl.program_id(1), 0)

    def sublane_max(s):
        return jnp.max(s.reshape(blk // 8, 8, s.shape[1]), axis=0)

    def scores(c, col_max):
        n = c * blk
        qt = q_ref[0, n:n + blk, :].astype(F32).T.astype(BF16)
        q_cols = jnp.concatenate([jnp.where(feat < HEAD_DIM, qt, zero), jnp.where(feat >= HEAD_DIM, qt, zero)],
                                 axis=1)
        s_diag = _dot(k_ref[0, n:n + blk, :], q_cols) + bias_diag_ref[...]
        s_slot[c % n_slots][n:n + blk, :] = s_diag
        m = sublane_max(s_diag)
        yield
        if c > 0:
            terms = (pieces(slope * float(1 << POS_LOW_BITS)) + pieces(slope)
                     + pieces(-slope * (q_off + float(n))))
            bias_cols = jnp.zeros(bias_row.shape, F32)
            for r, term in enumerate(terms):
                bias_cols = jnp.where(bias_row == r, term, bias_cols)
            w = jnp.concatenate([q_cols, bias_cols.astype(BF16),
                                 jnp.zeros((HEAD_LANES - BIAS_ROWS, 2 * blk), BF16)], axis=0)
            for j in range(c):
                rows = slice(j * blk, (j + 1) * blk)
                s_past = _dot(jnp.concatenate([k_ref[0, rows, :], key_pos_ref[rows, :]], axis=1), w)
                s_slot[c % n_slots][rows, :] = s_past
                m = jnp.maximum(m, sublane_max(s_past))
                yield
        col_max[c] = jnp.max(m, axis=0, keepdims=True)

    def outputs(c, m):
        n = c * blk
        acc = None
        for j in range(c + 1):
            rows = slice(j * blk, (j + 1) * blk)
            s = s_slot[c % n_slots][pl.ds(pl.multiple_of(run_time_zero + j * blk, blk), blk), :]
            pv = _dot(vt_ref[0, :, rows], jnp.exp2(s - m).astype(BF16))
            acc = pv if acc is None else acc + pv
            yield
        o = [acc[0:V_DIM, half * blk:(half + 1) * blk] / acc[V_DIM:V_DIM + 1, half * blk:(half + 1) * blk]
             for half in range(2)]
        ot = o[0] - lam * o[1]
        ms = jnp.mean(ot * ot, axis=0, keepdims=True)
        yt = (ot * lax.rsqrt(ms + SUBLN_EPS) * gcol_ref[...]) * (1.0 - lambda_init)
        o_ref[0, n:n + blk, :] = yt.T.astype(o_ref.dtype)

    col_max = {}
    for _ in scores(n_blocks - 1, col_max):
        pass
    for c in reversed(range(n_blocks)):
        gens = [outputs(c, col_max[c])]
        if c > 0:
            gens.insert(0, scores(c - 1, col_max))
        while gens:
            for g in list(gens):
                for _ in range(INTERLEAVE_GROUP):
                    if next(g, StopIteration) is StopIteration:
                        gens.remove(g)
                        break


def _prompt_attn(qb, kb, vtb, lams, subln_gcol, *, lambda_init, blk):
    B, T, _ = qb.shape
    grid = (N_HEADS, B)
    vec = lambda n: pl.BlockSpec((1, n), lambda h, b: (0, 0))
    qk_spec = pl.BlockSpec((1, T, HEAD_LANES), lambda h, b: (b, 0, h))
    vt_spec = pl.BlockSpec((1, VT_ROWS, T), lambda h, b: (b, h, 0))
    return pl.pallas_call(
        functools.partial(_prompt_attn_kernel, lambda_init=lambda_init, blk=blk),
        grid=grid,
        in_specs=[vec(HEAD_DIM)] * 4 + [pl.BlockSpec((V_DIM, 1), lambda h, b: (0, 0)), qk_spec, qk_spec, vt_spec],
        out_specs=qk_spec,
        out_shape=jax.ShapeDtypeStruct((B, T, V_WIDTH), BF16),
        scratch_shapes=[pltpu.VMEM((T - blk, HEAD_LANES), BF16), pltpu.VMEM((blk, 2 * blk), F32)]
                       + [pltpu.VMEM((T, 2 * blk), F32)] * SCORE_SLOTS,
        compiler_params=pltpu.CompilerParams(dimension_semantics=("arbitrary",) * 2,
                                             vmem_limit_bytes=VMEM_LIMIT),
        name="prompt_attn",
    )(*lams, subln_gcol, qb, kb, vtb)


def _sample_attn_step(j, nj, lq1, lk1, lq2, lk2, g_ref, q_ref, kn_ref, vn_ref, ckt_ref, cv_ref, o_ref,
                      qrows_ref, m_ref, l_ref, acc_ref, *, lambda_init, past_len):
    tq = q_ref.shape[1]
    head_rows = 2 * tq
    n_rows = N_HEADS * head_rows
    kb = ckt_ref.shape[2]

    tq_bits = tq.bit_length() - 1
    assert tq == 1 << tq_bits
    row = lax.broadcasted_iota(jnp.int32, (n_rows, 1), 0)
    q_rel = row & (tq - 1)
    slope = _head_slope(row >> (tq_bits + 1))

    @pl.when(j == 0)
    def _():
        q = q_ref[0]
        tiled = jnp.concatenate([q] * (2 * N_HEADS), axis=0)
        r = lax.broadcasted_iota(jnp.int32, tiled.shape, 0)
        c = lax.broadcasted_iota(jnp.int32, tiled.shape, 1)
        qrows_ref[...] = jnp.where((r >> tq_bits) == (c >> HEAD_DIM_BITS), tiled, jnp.zeros_like(tiled))
        m_ref[...] = jnp.full(m_ref.shape, -jnp.inf, F32)
        l_ref[...] = jnp.zeros(l_ref.shape, F32)
        acc_ref[...] = jnp.zeros(acc_ref.shape, F32)

    def update(s, values):
        m_old = m_ref[...]
        m_new = jnp.maximum(m_old, jnp.max(s, axis=-1, keepdims=True))
        p = jnp.exp2(s - m_new)
        alpha = jnp.exp2(m_old - m_new)
        l_ref[...] = alpha * l_ref[...] + jnp.sum(p, axis=-1, keepdims=True)
        m_ref[...] = m_new
        acc_ref[...] = alpha * acc_ref[...] + _dot(p.astype(BF16), values)

    scores = []

    def score_cache():
        k_rel = (j * kb - past_len + lax.broadcasted_iota(jnp.int32, (1, kb), 1)).astype(F32)
        scores.append(_dot(qrows_ref[...], ckt_ref[0].astype(BF16)) + slope * k_rel)

    def attend_cache():
        values = jnp.concatenate(
            [cv_ref[0, pl.ds(h, kb, stride=N_HEADS), :].astype(BF16) for h in range(N_HEADS)], axis=1)
        update(scores.pop(), values)

    def finish_stream():
        @pl.when(j == nj - 1)
        def _():
            k_new = lax.broadcasted_iota(jnp.int32, (1, tq), 1)
            update(_nt_dot(qrows_ref[...], kn_ref[0]) + slope * (q_rel - jnp.abs(q_rel - k_new)).astype(F32),
                   vn_ref[0])
            lam = _lambda(lq1, lk1, lq2, lk2, lambda_init)
            out = acc_ref[...] / l_ref[...]
            for h in range(N_HEADS):
                own = out[h * head_rows:(h + 1) * head_rows, h * V_DIM:(h + 1) * V_DIM]
                o_ref[0, :, h * V_DIM:(h + 1) * V_DIM] = _sub_ln(own[:tq], own[tq:], lam, g_ref[...],
                                                                 lambda_init).astype(o_ref.dtype)

    return score_cache, attend_cache, finish_stream


def _out_mlp_kernel(pool_ref, attn_ref, x_ref, w_out_ref, gf_ref, w_up_ref, w_down_ref, gl_ref, y_ref):
    _out_mlp_tile(pool_ref, attn_ref, x_ref, w_out_ref, gf_ref, w_up_ref, w_down_ref, gl_ref, y_ref)


def _out_mlp_tile(pool_ref, attn_ref, x_ref, w_out_ref, gf_ref, w_up_ref, w_down_ref, gl_ref, y_ref,
                  after_chunk=()):
    mixed = (_dot(pool_ref[...], w_out_ref[0:POOL_WIDTH, :].astype(BF16))
             + _dot(attn_ref[...], w_out_ref[POOL_WIDTH:, :].astype(BF16)))
    h = x_ref[...] + mixed
    ms = jnp.mean(h * h, axis=-1, keepdims=True)
    hn = (h * lax.rsqrt(ms + EPS) * gf_ref[...]).astype(BF16)
    y = h
    for c in range(D_FF // FF_CHUNK):
        cols = slice(c * FF_CHUNK, (c + 1) * FF_CHUNK)
        a = jnp.maximum(_dot(hn, w_up_ref[:, cols]), 0.0)
        y = y + _dot((a * a).astype(BF16), w_down_ref[cols, :])
        if c < len(after_chunk):
            after_chunk[c]()
    ms = jnp.mean(y * y, axis=-1, keepdims=True)
    y_ref[...] = y * lax.rsqrt(ms + EPS) * gl_ref[...]


def _out_mlp(pool, attn, x, w_out, gf, w_up, w_down, gl, *, tm):
    rows = x.shape[0]
    row_spec = lambda width: pl.BlockSpec((tm, width), lambda r: (r, 0))
    const = lambda shape: pl.BlockSpec(shape, lambda r: (0, 0), pipeline_mode=pl.Buffered(1))
    return pl.pallas_call(
        _out_mlp_kernel,
        grid=(rows // tm,),
        in_specs=[row_spec(POOL_WIDTH), row_spec(V_WIDTH), row_spec(D_MODEL),
                  const((D_MODEL, D_MODEL)), const((1, D_MODEL)), const((D_MODEL, D_FF)),
                  const((D_FF, D_MODEL)), const((1, D_MODEL))],
        out_specs=row_spec(D_MODEL),
        out_shape=jax.ShapeDtypeStruct((rows, D_MODEL), F32),
        compiler_params=pltpu.CompilerParams(dimension_semantics=("arbitrary",),
                                             vmem_limit_bytes=VMEM_LIMIT),
        name="out_mlp",
    )(pool, attn, x, w_out, gf, w_up, w_down, gl)


def _out_mlp_sample_attn_kernel(*refs, parts, lambda_init, past_len):
    mlp_in, samp_in, (y_ref, o_ref), scratch = refs[:8], refs[8:18], refs[18:20], refs[20:]
    r = pl.program_id(0)
    score_cache, attend_cache, finish_stream = _sample_attn_step(
        r % parts, parts, *samp_in, o_ref, *scratch, lambda_init=lambda_init, past_len=past_len)
    _out_mlp_tile(*mlp_in, y_ref, after_chunk=(score_cache, attend_cache))
    finish_stream()


def _out_mlp_with_sample_attn(pool, attn, x, w_out, gf, w_up, w_down, gl,
                              qb, kb_new, vb_new, cache_kt, cache_v, lams, subln_g, *, tm, lambda_init):
    rows = x.shape[0]
    S, tq, _ = qb.shape
    past_len = cache_kt.shape[2]
    steps = rows // tm
    assert steps % S == 0, "every stream gets the same number of grid steps"
    parts = steps // S
    kblk = past_len // parts
    n_rows = 2 * N_HEADS * tq
    row_spec = lambda width: pl.BlockSpec((tm, width), lambda r: (r, 0))
    const = lambda shape: pl.BlockSpec(shape, lambda r: (0, 0), pipeline_mode=pl.Buffered(1))
    vec = lambda n: pl.BlockSpec((1, n), lambda r: (0, 0))
    new_spec = pl.BlockSpec((1, tq, QK_WIDTH), lambda r: (r // parts, 0, 0))
    kt_spec = pl.BlockSpec((1, QK_WIDTH, kblk), lambda r: (r // parts, 0, r % parts))
    v_spec = pl.BlockSpec((1, kblk * N_HEADS, V_DIM), lambda r: (r // parts, r % parts, 0))
    return pl.pallas_call(
        functools.partial(_out_mlp_sample_attn_kernel, parts=parts, lambda_init=lambda_init, past_len=past_len),
        grid=(steps,),
        in_specs=[row_spec(POOL_WIDTH), row_spec(V_WIDTH), row_spec(D_MODEL),
                  const((D_MODEL, D_MODEL)), const((1, D_MODEL)), const((D_MODEL, D_FF)),
                  const((D_FF, D_MODEL)), const((1, D_MODEL))]
                 + [vec(HEAD_DIM)] * 4 + [vec(V_DIM), new_spec, new_spec, new_spec, kt_spec, v_spec],
        out_specs=[row_spec(D_MODEL), new_spec],
        out_shape=[jax.ShapeDtypeStruct((rows, D_MODEL), F32), jax.ShapeDtypeStruct((S, tq, V_WIDTH), BF16)],
        scratch_shapes=[pltpu.VMEM((n_rows, QK_WIDTH), BF16), pltpu.VMEM((n_rows, 1), F32),
                        pltpu.VMEM((n_rows, 1), F32), pltpu.VMEM((n_rows, V_WIDTH), F32)],
        compiler_params=pltpu.CompilerParams(dimension_semantics=("arbitrary",),
                                             vmem_limit_bytes=VMEM_LIMIT),
        name="out_mlp_sample_attn",
    )(pool, attn, x, w_out, gf, w_up, w_down, gl, *lams, subln_g, qb, kb_new, vb_new, cache_kt, cache_v)


def _keys_from_feature_major(kt, frames):
    streams = kt.shape[0]
    return jnp.transpose(kt.reshape(streams, N_HEADS, 2, HEAD_DIM, frames), (0, 4, 1, 2, 3))[None]


def kernel(x_prompt, x_sample, state_pool, cache_k, cache_v, norm_mix_g, w_in, w_pool, pool_scale,
           lambda_q1, lambda_k1, lambda_q2, lambda_k2, subln_g, w_out, norm_ffn_g, w_up, w_down,
           norm_final_g):
    assert w_in.shape[0] == 1, "one layer per call"
    B, T, _ = x_prompt.shape
    S, TS, _ = x_sample.shape
    past_len = cache_k.shape[2]
    lambda_init = 0.8 - 0.6 * math.exp(-0.3 * 0)

    g_mix = norm_mix_g[0][None]
    g_ffn = norm_ffn_g[0][None]
    g_fin = norm_final_g[None]
    g_sub = subln_g[0][None]
    pscale = pool_scale[0][None]
    lams = (lambda_q1[0][None], lambda_k1[0][None], lambda_q2[0][None], lambda_k2[0][None])
    w_in_f, w_pool_f, w_out_f = w_in[0], w_pool[0], w_out[0]
    w_up_b = w_up[0].astype(BF16)
    w_down_b = w_down[0].astype(BF16)

    zero_buf = jnp.zeros((B, TAIL, POOL_WIDTH), F32)
    kt_p, v_p, qb, kb, vtb, pool_p, tail_p = _proj_pool(
        x_prompt, zero_buf, g_mix, w_in_f, w_pool_f, pscale, pos0=0, bb=1, tm=PROJ_ROWS,
        feature_major=True)
    attn_p = _prompt_attn(qb, kb, vtb, lams, g_sub.reshape(V_DIM, 1), lambda_init=lambda_init, blk=ATTN_BLOCK)

    buf_s = jnp.pad(state_pool[0], ((0, 0), (TAIL - POOL_STATE, 0), (0, 0)))
    k_s, v_s, qb_s, kb_s, vb_s, pool_s, tail_s = _proj_pool(
        x_sample, buf_s, g_mix, w_in_f, w_pool_f, pscale, pos0=past_len, bb=S, tm=TS,
        feature_major=False)
    cache_kt = jnp.transpose(cache_k[0], (0, 2, 3, 4, 1)).reshape(S, QK_WIDTH, past_len)
    cache_vi = cache_v[0].reshape(S, past_len * N_HEADS, V_DIM)

    y_p, attn_s = _out_mlp_with_sample_attn(
        pool_p.reshape(B * T, -1), attn_p.reshape(B * T, -1), x_prompt.reshape(B * T, -1),
        w_out_f, g_ffn, w_up_b, w_down_b, g_fin, qb_s, kb_s, vb_s, cache_kt, cache_vi, lams, g_sub,
        tm=PROMPT_ROWS, lambda_init=lambda_init)
    y_p = y_p.reshape(B, T, D_MODEL)
    y_s = _out_mlp(pool_s.reshape(S * TS, -1), attn_s.reshape(S * TS, -1), x_sample.reshape(S * TS, -1),
                   w_out_f, g_ffn, w_up_b, w_down_b, g_fin, tm=S * TS).reshape(S, TS, D_MODEL)

    return (y_p, y_s,
            tail_p[:, TAIL - POOL_STATE:][None],
            _keys_from_feature_major(kt_p, T),
            v_p.reshape(1, B, T, N_HEADS, V_DIM),
            tail_s[:, TAIL - POOL_STATE:][None],
            k_s.reshape(1, S, TS, N_HEADS, 2, HEAD_DIM),
            v_s.reshape(1, S, TS, N_HEADS, V_DIM))
```

```python
import functools
import math

import jax
import jax.numpy as jnp
from jax import lax
from jax.experimental import pallas as pl
from jax.experimental.pallas import tpu as pltpu

D_MODEL = 1024
POOL_WIDTH = 512
POOL_WINDOWS = (2, 4, 8, 16)
POOL_GROUP = 128
POOL_STATE = 15
N_HEADS = 4
HEAD_DIM = 64
V_DIM = 128
QK_WIDTH = 512
V_WIDTH = N_HEADS * V_DIM
D_FF = 4096
CHUNK_BITS = 6
HEAD_DIM_BITS = 6
assert HEAD_DIM == 1 << HEAD_DIM_BITS
EPS = 1e-6
SUBLN_EPS = 1e-5

TAIL = 16
HEAD_LANES = 2 * HEAD_DIM
SUM_ROWS = 16
VT_ROWS = V_DIM + SUM_ROWS
BIAS_TERMS = 3
BIAS_ROWS = 16
assert 3 * BIAS_TERMS <= BIAS_ROWS
POS_LOW_BITS = 8

BF16 = jnp.bfloat16
F32 = jnp.float32
LOG2_E = math.log2(math.e)

PROJ_ROWS = 1024
PROMPT_ROWS = 512
ATTN_BLOCK = 256
FF_CHUNK = 1024
INTERLEAVE_GROUP = 1
SCORE_SLOTS = 3
VMEM_LIMIT = 56 * 1024 * 1024


def _nt_dot(a, b):
    return lax.dot_general(a, b, (((1,), (1,)), ((), ())), preferred_element_type=F32)


def _dot(a, b):
    return jnp.dot(a, b, preferred_element_type=F32)


def _head_slope(h):
    return lax.bitcast_convert_type((127 - 2 * (h + 1)) << 23, F32) * LOG2_E


def _lambda(lq1, lk1, lq2, lk2, lambda_init):
    return (jnp.exp(jnp.sum(lq1[...] * lk1[...], axis=-1, keepdims=True))
            - jnp.exp(jnp.sum(lq2[...] * lk2[...], axis=-1, keepdims=True)) + lambda_init)


def _sub_ln(o0, o1, lam, g, lambda_init):
    o = o0 - lam * o1
    ms = jnp.mean(o * o, axis=-1, keepdims=True)
    return (o * lax.rsqrt(ms + SUBLN_EPS) * g) * (1.0 - lambda_init)


def _proj_pool_kernel(x_ref, g_ref, w_in_ref, w_pool_ref, pscale_ref, buf_ref,
                      k_ref, v_ref, qb_ref, kb_ref, vb_ref, pool_ref, tail_ref, ext_ref,
                      *, pos0, feature_major):
    bb, tm, _ = x_ref.shape
    rows = bb * tm
    t = pl.program_id(1)

    @pl.when(t == 0)
    def _():
        ext_ref[:, 0:TAIL, :] = buf_ref[...]

    x = x_ref[...].reshape(rows, D_MODEL)
    ms = jnp.mean(x * x, axis=-1, keepdims=True)
    hn = (x * lax.rsqrt(ms + EPS) * g_ref[...]).astype(BF16)

    def proj(j):
        return _dot(hn, w_in_ref[:, j * 512:(j + 1) * 512].astype(BF16))

    ext_ref[:, TAIL:TAIL + tm, :] = proj(0).reshape(bb, tm, POOL_WIDTH)
    pos = pos0 + t * tm + lax.broadcasted_iota(jnp.int32, (1, tm, 1), 1)

    def pool_group(g):
        w = POOL_WINDOWS[g]
        lanes = slice(g * POOL_GROUP, (g + 1) * POOL_GROUP)
        ext = ext_ref[:, :, lanes]
        acc, span = ext, 1
        while span < w:
            acc = acc + pltpu.roll(acc, span, axis=1)
            span *= 2
        cnt = jnp.minimum(pos + 1, w).astype(F32)
        diff = (acc[:, TAIL:, :] / cnt - ext[:, TAIL:, :]).reshape(rows, POOL_GROUP).astype(BF16)
        y = _dot(diff, w_pool_ref[g].astype(BF16)) * pscale_ref[:, lanes]
        pool_ref[:, :, lanes] = y.astype(BF16).reshape(bb, tm, POOL_GROUP)

    k = proj(2)
    kb_ref[...] = k.astype(BF16).reshape(bb, tm, QK_WIDTH)
    if feature_major:
        assert bb == 1
        k_ref[0] = k.T
    else:
        k_ref[...] = k.reshape(bb, tm, QK_WIDTH)
    pool_group(0)
    pool_group(1)
    v = proj(3)
    if feature_major:
        vt = v.T.astype(BF16)
        for h in range(N_HEADS):
            vb_ref[0, h * VT_ROWS:h * VT_ROWS + V_DIM, :] = vt[h * V_DIM:(h + 1) * V_DIM]
            vb_ref[0, h * VT_ROWS + V_DIM:(h + 1) * VT_ROWS, :] = jnp.ones((SUM_ROWS, tm), BF16)
    else:
        vb_ref[...] = v.astype(BF16).reshape(bb, tm, V_WIDTH)
    for h in range(N_HEADS):
        v_ref[:, pl.ds(h, tm, stride=N_HEADS), :] = v[:, h * V_DIM:(h + 1) * V_DIM].reshape(bb, tm, V_DIM)
    pool_group(2)
    pool_group(3)
    q = proj(1)
    qb_ref[...] = (q * (LOG2_E / math.sqrt(HEAD_DIM))).astype(BF16).reshape(bb, tm, QK_WIDTH)

    tail = ext_ref[:, tm:tm + TAIL, :]
    tail_ref[...] = tail
    ext_ref[:, 0:TAIL, :] = tail


def _proj_pool(x, buf, g, w_in, w_pool, pscale, *, pos0, bb, tm, feature_major):
    B, T, _ = x.shape
    grid = (B // bb, T // tm)
    row_spec = lambda width: pl.BlockSpec((bb, tm, width), lambda b, t: (b, t, 0))
    const = lambda shape: pl.BlockSpec(shape, lambda b, t: (0,) * len(shape), pipeline_mode=pl.Buffered(1))
    seq_spec = pl.BlockSpec((bb, TAIL, POOL_WIDTH), lambda b, t: (b, 0, 0))
    act = lambda dtype: jax.ShapeDtypeStruct((B, T, QK_WIDTH), dtype)
    if feature_major:
        fm_spec = lambda rows: pl.BlockSpec((bb, rows, tm), lambda b, t: (b, 0, t))
        fm_shape = lambda rows, dtype: jax.ShapeDtypeStruct((B, rows, T), dtype)
        k_spec, k_shape = fm_spec(QK_WIDTH), fm_shape(QK_WIDTH, F32)
        vb_spec, vb_shape = fm_spec(N_HEADS * VT_ROWS), fm_shape(N_HEADS * VT_ROWS, BF16)
    else:
        k_spec, k_shape = row_spec(QK_WIDTH), act(F32)
        vb_spec, vb_shape = row_spec(V_WIDTH), act(BF16)
    return pl.pallas_call(
        functools.partial(_proj_pool_kernel, pos0=pos0, feature_major=feature_major),
        grid=grid,
        in_specs=[row_spec(D_MODEL), const((1, D_MODEL)), const((D_MODEL, 4 * 512)),
                  const((len(POOL_WINDOWS), POOL_GROUP, POOL_GROUP)), const((1, POOL_WIDTH)), seq_spec],
        out_specs=[k_spec, pl.BlockSpec((bb, tm * N_HEADS, V_DIM), lambda b, t: (b, t, 0)),
                   row_spec(QK_WIDTH), row_spec(QK_WIDTH), vb_spec, row_spec(POOL_WIDTH), seq_spec],
        out_shape=[k_shape, jax.ShapeDtypeStruct((B, T * N_HEADS, V_DIM), F32),
                   act(BF16), act(BF16), vb_shape, act(BF16),
                   jax.ShapeDtypeStruct((B, TAIL, POOL_WIDTH), F32)],
        scratch_shapes=[pltpu.VMEM((bb, TAIL + tm, POOL_WIDTH), F32)],
        compiler_params=pltpu.CompilerParams(dimension_semantics=("arbitrary", "arbitrary"),
                                             vmem_limit_bytes=VMEM_LIMIT),
        name="proj_pool",
    )(x, g, w_in, w_pool, pscale, buf)


def _prompt_attn_kernel(lq1, lk1, lq2, lk2, gcol_ref, q_ref, k_ref, vt_ref, o_ref,
                        key_pos_ref, bias_diag_ref, *s_slot, lambda_init, blk):
    n_blocks = q_ref.shape[1] // blk
    h = pl.program_id(0)
    slope = _head_slope(jnp.full((1, 1), h, jnp.int32))

    @pl.when((h == 0) & (pl.program_id(1) == 0))
    def _():
        kg = lax.broadcasted_iota(jnp.int32, key_pos_ref.shape, 0)
        lane = lax.broadcasted_iota(jnp.int32, key_pos_ref.shape, 1)
        code = jnp.where(lane < BIAS_TERMS, kg >> POS_LOW_BITS,
                         jnp.where(lane < 2 * BIAS_TERMS, kg & ((1 << POS_LOW_BITS) - 1),
                                   jnp.where(lane < 3 * BIAS_TERMS, 1, 0)))
        key_pos_ref[...] = code.astype(F32).astype(BF16)

    @pl.when(pl.program_id(1) == 0)
    def _():
        kk = lax.broadcasted_iota(jnp.int32, bias_diag_ref.shape, 0)
        qq = lax.broadcasted_iota(jnp.int32, bias_diag_ref.shape, 1) & (blk - 1)
        visible = (kk >> CHUNK_BITS) <= (qq >> CHUNK_BITS)
        bias_diag_ref[...] = jnp.where(visible, -slope * jnp.abs(qq - kk).astype(F32), -jnp.inf)

    lam = _lambda(lq1, lk1, lq2, lk2, lambda_init)
    feat = lax.broadcasted_iota(jnp.int32, (HEAD_LANES, blk), 0)
    zero = jnp.zeros((HEAD_LANES, blk), BF16)
    q_off = (lax.broadcasted_iota(jnp.int32, (1, 2 * blk), 1) & (blk - 1)).astype(F32)
    bias_row = lax.broadcasted_iota(jnp.int32, (BIAS_ROWS, 2 * blk), 0)

    def pieces(x):
        out = []
        for _ in range(BIAS_TERMS):
            piece = x.astype(BF16).astype(F32)
            out.append(piece)
            x = x - piece
        return out

    n_slots = len(s_slot)
    run_time_zero = jnp.minimum(pl.program_id(1), 0)

    def sublane_max(s):
        return jnp.max(s.reshape(blk // 8, 8, s.shape[1]), axis=0)

    def scores(c, col_max):
        n = c * blk
        qt = q_ref[0, n:n + blk, :].astype(F32).T.astype(BF16)
        q_cols = jnp.concatenate([jnp.where(feat < HEAD_DIM, qt, zero), jnp.where(feat >= HEAD_DIM, qt, zero)],
                                 axis=1)
        s_diag = _dot(k_ref[0, n:n + blk, :], q_cols) + bias_diag_ref[...]
        s_slot[c % n_slots][n:n + blk, :] = s_diag
        m = sublane_max(s_diag)
        yield
        if c > 0:
            terms = (pieces(slope * float(1 << POS_LOW_BITS)) + pieces(slope)
                     + pieces(-slope * (q_off + float(n))))
            bias_cols = jnp.zeros(bias_row.shape, F32)
            for r, term in enumerate(terms):
                bias_cols = jnp.where(bias_row == r, term, bias_cols)
            w = jnp.concatenate([q_cols, bias_cols.astype(BF16),
                                 jnp.zeros((HEAD_LANES - BIAS_ROWS, 2 * blk), BF16)], axis=0)
            for j in range(c):
                rows = slice(j * blk, (j + 1) * blk)
                s_past = _dot(jnp.concatenate([k_ref[0, rows, :], key_pos_ref[rows, :]], axis=1), w)
                s_slot[c % n_slots][rows, :] = s_past
                m = jnp.maximum(m, sublane_max(s_past))
                yield
        col_max[c] = jnp.max(m, axis=0, keepdims=True)

    def outputs(c, m):
        n = c * blk
        acc = None
        for j in range(c + 1):
            rows = slice(j * blk, (j + 1) * blk)
            s = s_slot[c % n_slots][pl.ds(pl.multiple_of(run_time_zero + j * blk, blk), blk), :]
            pv = _dot(vt_ref[0, :, rows], jnp.exp2(s - m).astype(BF16))
            acc = pv if acc is None else acc + pv
            yield
        o = [acc[0:V_DIM, half * blk:(half + 1) * blk] / acc[V_DIM:V_DIM + 1, half * blk:(half + 1) * blk]
             for half in range(2)]
        ot = o[0] - lam * o[1]
        ms = jnp.mean(ot * ot, axis=0, keepdims=True)
        yt = (ot * lax.rsqrt(ms + SUBLN_EPS) * gcol_ref[...]) * (1.0 - lambda_init)
        o_ref[0, n:n + blk, :] = yt.T.astype(o_ref.dtype)

    col_max = {}
    ahead = SCORE_SLOTS - 1
    for c in range(n_blocks - 1, max(n_blocks - 1 - ahead, -1), -1):
        for _ in scores(c, col_max):
            pass
    for c in reversed(range(n_blocks)):
        gens = [outputs(c, col_max[c])]
        if c - ahead >= 0:
            gens.insert(0, scores(c - ahead, col_max))
        while gens:
            for g in list(gens):
                for _ in range(INTERLEAVE_GROUP):
                    if next(g, StopIteration) is StopIteration:
                        gens.remove(g)
                        break


def _prompt_attn(qb, kb, vtb, lams, subln_gcol, *, lambda_init, blk):
    B, T, _ = qb.shape
    grid = (N_HEADS, B)
    vec = lambda n: pl.BlockSpec((1, n), lambda h, b: (0, 0))
    qk_spec = pl.BlockSpec((1, T, HEAD_LANES), lambda h, b: (b, 0, h))
    vt_spec = pl.BlockSpec((1, VT_ROWS, T), lambda h, b: (b, h, 0))
    return pl.pallas_call(
        functools.partial(_prompt_attn_kernel, lambda_init=lambda_init, blk=blk),
        grid=grid,
        in_specs=[vec(HEAD_DIM)] * 4 + [pl.BlockSpec((V_DIM, 1), lambda h, b: (0, 0)), qk_spec, qk_spec, vt_spec],
        out_specs=qk_spec,
        out_shape=jax.ShapeDtypeStruct((B, T, V_WIDTH), BF16),
        scratch_shapes=[pltpu.VMEM((T - blk, HEAD_LANES), BF16), pltpu.VMEM((blk, 2 * blk), F32)]
                       + [pltpu.VMEM((T, 2 * blk), F32)] * SCORE_SLOTS,
        compiler_params=pltpu.CompilerParams(dimension_semantics=("arbitrary",) * 2,
                                             vmem_limit_bytes=VMEM_LIMIT),
        name="prompt_attn",
    )(*lams, subln_gcol, qb, kb, vtb)


def _sample_attn_step(j, nj, lq1, lk1, lq2, lk2, g_ref, q_ref, kn_ref, vn_ref, ckt_ref, cv_ref, o_ref,
                      qrows_ref, m_ref, l_ref, acc_ref, *, lambda_init, past_len):
    tq = q_ref.shape[1]
    head_rows = 2 * tq
    n_rows = N_HEADS * head_rows
    kb = ckt_ref.shape[2]

    tq_bits = tq.bit_length() - 1
    assert tq == 1 << tq_bits
    row = lax.broadcasted_iota(jnp.int32, (n_rows, 1), 0)
    q_rel = row & (tq - 1)
    slope = _head_slope(row >> (tq_bits + 1))

    @pl.when(j == 0)
    def _():
        q = q_ref[0]
        tiled = jnp.concatenate([q] * (2 * N_HEADS), axis=0)
        r = lax.broadcasted_iota(jnp.int32, tiled.shape, 0)
        c = lax.broadcasted_iota(jnp.int32, tiled.shape, 1)
        qrows_ref[...] = jnp.where((r >> tq_bits) == (c >> HEAD_DIM_BITS), tiled, jnp.zeros_like(tiled))
        m_ref[...] = jnp.full(m_ref.shape, -jnp.inf, F32)
        l_ref[...] = jnp.zeros(l_ref.shape, F32)
        acc_ref[...] = jnp.zeros(acc_ref.shape, F32)

    def update(s, values):
        m_old = m_ref[...]
        m_new = jnp.maximum(m_old, jnp.max(s, axis=-1, keepdims=True))
        p = jnp.exp2(s - m_new)
        alpha = jnp.exp2(m_old - m_new)
        l_ref[...] = alpha * l_ref[...] + jnp.sum(p, axis=-1, keepdims=True)
        m_ref[...] = m_new
        acc_ref[...] = alpha * acc_ref[...] + _dot(p.astype(BF16), values)

    scores = []

    def score_cache():
        k_rel = (j * kb - past_len + lax.broadcasted_iota(jnp.int32, (1, kb), 1)).astype(F32)
        scores.append(_dot(qrows_ref[...], ckt_ref[0].astype(BF16)) + slope * k_rel)

    def attend_cache():
        values = jnp.concatenate(
            [cv_ref[0, pl.ds(h, kb, stride=N_HEADS), :].astype(BF16) for h in range(N_HEADS)], axis=1)
        update(scores.pop(), values)

    def finish_stream():
        @pl.when(j == nj - 1)
        def _():
            k_new = lax.broadcasted_iota(jnp.int32, (1, tq), 1)
            update(_nt_dot(qrows_ref[...], kn_ref[0]) + slope * (q_rel - jnp.abs(q_rel - k_new)).astype(F32),
                   vn_ref[0])
            lam = _lambda(lq1, lk1, lq2, lk2, lambda_init)
            out = acc_ref[...] / l_ref[...]
            for h in range(N_HEADS):
                own = out[h * head_rows:(h + 1) * head_rows, h * V_DIM:(h + 1) * V_DIM]
                o_ref[0, :, h * V_DIM:(h + 1) * V_DIM] = _sub_ln(own[:tq], own[tq:], lam, g_ref[...],
                                                                 lambda_init).astype(o_ref.dtype)

    return score_cache, attend_cache, finish_stream


def _out_mlp_kernel(pool_ref, attn_ref, x_ref, w_out_ref, gf_ref, w_up_ref, w_down_ref, gl_ref, y_ref):
    _out_mlp_tile(pool_ref, attn_ref, x_ref, w_out_ref, gf_ref, w_up_ref, w_down_ref, gl_ref, y_ref)


def _out_mlp_tile(pool_ref, attn_ref, x_ref, w_out_ref, gf_ref, w_up_ref, w_down_ref, gl_ref, y_ref,
                  after_chunk=()):
    mixed = (_dot(pool_ref[...], w_out_ref[0:POOL_WIDTH, :].astype(BF16))
             + _dot(attn_ref[...], w_out_ref[POOL_WIDTH:, :].astype(BF16)))
    h = x_ref[...] + mixed
    ms = jnp.mean(h * h, axis=-1, keepdims=True)
    hn = (h * lax.rsqrt(ms + EPS) * gf_ref[...]).astype(BF16)
    y = h
    for c in range(D_FF // FF_CHUNK):
        cols = slice(c * FF_CHUNK, (c + 1) * FF_CHUNK)
        a = jnp.maximum(_dot(hn, w_up_ref[:, cols]), 0.0)
        y = y + _dot((a * a).astype(BF16), w_down_ref[cols, :])
        if c < len(after_chunk):
            after_chunk[c]()
    ms = jnp.mean(y * y, axis=-1, keepdims=True)
    y_ref[...] = y * lax.rsqrt(ms + EPS) * gl_ref[...]


def _out_mlp(pool, attn, x, w_out, gf, w_up, w_down, gl, *, tm):
    rows = x.shape[0]
    row_spec = lambda width: pl.BlockSpec((tm, width), lambda r: (r, 0))
    const = lambda shape: pl.BlockSpec(shape, lambda r: (0, 0), pipeline_mode=pl.Buffered(1))
    return pl.pallas_call(
        _out_mlp_kernel,
        grid=(rows // tm,),
        in_specs=[row_spec(POOL_WIDTH), row_spec(V_WIDTH), row_spec(D_MODEL),
                  const((D_MODEL, D_MODEL)), const((1, D_MODEL)), const((D_MODEL, D_FF)),
                  const((D_FF, D_MODEL)), const((1, D_MODEL))],
        out_specs=row_spec(D_MODEL),
        out_shape=jax.ShapeDtypeStruct((rows, D_MODEL), F32),
        compiler_params=pltpu.CompilerParams(dimension_semantics=("arbitrary",),
                                             vmem_limit_bytes=VMEM_LIMIT),
        name="out_mlp",
    )(pool, attn, x, w_out, gf, w_up, w_down, gl)


def _out_mlp_sample_attn_kernel(*refs, parts, lambda_init, past_len):
    mlp_in, samp_in, (y_ref, o_ref), scratch = refs[:8], refs[8:18], refs[18:20], refs[20:]
    r = pl.program_id(0)
    score_cache, attend_cache, finish_stream = _sample_attn_step(
        r % parts, parts, *samp_in, o_ref, *scratch, lambda_init=lambda_init, past_len=past_len)
    _out_mlp_tile(*mlp_in, y_ref, after_chunk=(score_cache, attend_cache))
    finish_stream()


def _out_mlp_with_sample_attn(pool, attn, x, w_out, gf, w_up, w_down, gl,
                              qb, kb_new, vb_new, cache_kt, cache_v, lams, subln_g, *, tm, lambda_init):
    rows = x.shape[0]
    S, tq, _ = qb.shape
    past_len = cache_kt.shape[2]
    steps = rows // tm
    assert steps % S == 0, "every stream gets the same number of grid steps"
    parts = steps // S
    kblk = past_len // parts
    n_rows = 2 * N_HEADS * tq
    row_spec = lambda width: pl.BlockSpec((tm, width), lambda r: (r, 0))
    const = lambda shape: pl.BlockSpec(shape, lambda r: (0, 0), pipeline_mode=pl.Buffered(1))
    vec = lambda n: pl.BlockSpec((1, n), lambda r: (0, 0))
    new_spec = pl.BlockSpec((1, tq, QK_WIDTH), lambda r: (r // parts, 0, 0))
    kt_spec = pl.BlockSpec((1, QK_WIDTH, kblk), lambda r: (r // parts, 0, r % parts))
    v_spec = pl.BlockSpec((1, kblk * N_HEADS, V_DIM), lambda r: (r // parts, r % parts, 0))
    return pl.pallas_call(
        functools.partial(_out_mlp_sample_attn_kernel, parts=parts, lambda_init=lambda_init, past_len=past_len),
        grid=(steps,),
        in_specs=[row_spec(POOL_WIDTH), row_spec(V_WIDTH), row_spec(D_MODEL),
                  const((D_MODEL, D_MODEL)), const((1, D_MODEL)), const((D_MODEL, D_FF)),
                  const((D_FF, D_MODEL)), const((1, D_MODEL))]
                 + [vec(HEAD_DIM)] * 4 + [vec(V_DIM), new_spec, new_spec, new_spec, kt_spec, v_spec],
        out_specs=[row_spec(D_MODEL), new_spec],
        out_shape=[jax.ShapeDtypeStruct((rows, D_MODEL), F32), jax.ShapeDtypeStruct((S, tq, V_WIDTH), BF16)],
        scratch_shapes=[pltpu.VMEM((n_rows, QK_WIDTH), BF16), pltpu.VMEM((n_rows, 1), F32),
                        pltpu.VMEM((n_rows, 1), F32), pltpu.VMEM((n_rows, V_WIDTH), F32)],
        compiler_params=pltpu.CompilerParams(dimension_semantics=("arbitrary",),
                                             vmem_limit_bytes=VMEM_LIMIT),
        name="out_mlp_sample_attn",
    )(pool, attn, x, w_out, gf, w_up, w_down, gl, *lams, subln_g, qb, kb_new, vb_new, cache_kt, cache_v)


def _keys_from_feature_major(kt, frames):
    streams = kt.shape[0]
    return jnp.transpose(kt.reshape(streams, N_HEADS, 2, HEAD_DIM, frames), (0, 4, 1, 2, 3))[None]


def kernel(x_prompt, x_sample, state_pool, cache_k, cache_v, norm_mix_g, w_in, w_pool, pool_scale,
           lambda_q1, lambda_k1, lambda_q2, lambda_k2, subln_g, w_out, norm_ffn_g, w_up, w_down,
           norm_final_g):
    assert w_in.shape[0] == 1, "one layer per call"
    B, T, _ = x_prompt.shape
    S, TS, _ = x_sample.shape
    past_len = cache_k.shape[2]
    lambda_init = 0.8 - 0.6 * math.exp(-0.3 * 0)

    g_mix = norm_mix_g[0][None]
    g_ffn = norm_ffn_g[0][None]
    g_fin = norm_final_g[None]
    g_sub = subln_g[0][None]
    pscale = pool_scale[0][None]
    lams = (lambda_q1[0][None], lambda_k1[0][None], lambda_q2[0][None], lambda_k2[0][None])
    w_in_f, w_pool_f, w_out_f = w_in[0], w_pool[0], w_out[0]
    w_up_b = w_up[0].astype(BF16)
    w_down_b = w_down[0].astype(BF16)

    zero_buf = jnp.zeros((B, TAIL, POOL_WIDTH), F32)
    kt_p, v_p, qb, kb, vtb, pool_p, tail_p = _proj_pool(
        x_prompt, zero_buf, g_mix, w_in_f, w_pool_f, pscale, pos0=0, bb=1, tm=PROJ_ROWS,
        feature_major=True)
    attn_p = _prompt_attn(qb, kb, vtb, lams, g_sub.reshape(V_DIM, 1), lambda_init=lambda_init, blk=ATTN_BLOCK)

    buf_s = jnp.pad(state_pool[0], ((0, 0), (TAIL - POOL_STATE, 0), (0, 0)))
    k_s, v_s, qb_s, kb_s, vb_s, pool_s, tail_s = _proj_pool(
        x_sample, buf_s, g_mix, w_in_f, w_pool_f, pscale, pos0=past_len, bb=S, tm=TS,
        feature_major=False)
    cache_kt = jnp.transpose(cache_k[0], (0, 2, 3, 4, 1)).reshape(S, QK_WIDTH, past_len)
    cache_vi = cache_v[0].reshape(S, past_len * N_HEADS, V_DIM)

    y_p, attn_s = _out_mlp_with_sample_attn(
        pool_p.reshape(B * T, -1), attn_p.reshape(B * T, -1), x_prompt.reshape(B * T, -1),
        w_out_f, g_ffn, w_up_b, w_down_b, g_fin, qb_s, kb_s, vb_s, cache_kt, cache_vi, lams, g_sub,
        tm=PROMPT_ROWS, lambda_init=lambda_init)
    y_p = y_p.reshape(B, T, D_MODEL)
    y_s = _out_mlp(pool_s.reshape(S * TS, -1), attn_s.reshape(S * TS, -1), x_sample.reshape(S * TS, -1),
                   w_out_f, g_ffn, w_up_b, w_down_b, g_fin, tm=S * TS).reshape(S, TS, D_MODEL)

    return (y_p, y_s,
            tail_p[:, TAIL - POOL_STATE:][None],
            _keys_from_feature_major(kt_p, T),
            v_p.reshape(1, B, T, N_HEADS, V_DIM),
            tail_s[:, TAIL - POOL_STATE:][None],
            k_s.reshape(1, S, TS, N_HEADS, 2, HEAD_DIM),
            v_s.reshape(1, S, TS, N_HEADS, V_DIM))
```

```python
import functools
import math

import jax
import jax.numpy as jnp
from jax import lax
from jax.experimental import pallas as pl
from jax.experimental.pallas import tpu as pltpu

D_MODEL = 1024
POOL_WIDTH = 512
POOL_WINDOWS = (2, 4, 8, 16)
POOL_GROUP = 128
POOL_STATE = 15
N_HEADS = 4
HEAD_DIM = 64
V_DIM = 128
QK_WIDTH = 512
V_WIDTH = N_HEADS * V_DIM
D_FF = 4096
CHUNK_BITS = 6
HEAD_DIM_BITS = 6
assert HEAD_DIM == 1 << HEAD_DIM_BITS
EPS = 1e-6
SUBLN_EPS = 1e-5

TAIL = 16
HEAD_LANES = 2 * HEAD_DIM
SUM_ROWS = 16
VT_ROWS = V_DIM + SUM_ROWS
BIAS_TERMS = 3
BIAS_ROWS = 16
assert 3 * BIAS_TERMS <= BIAS_ROWS
POS_LOW_BITS = 8

BF16 = jnp.bfloat16
F32 = jnp.float32
LOG2_E = math.log2(math.e)

PROJ_ROWS = 1024
PROMPT_ROWS = 512
ATTN_BLOCK = 256
FF_CHUNK = 1024
INTERLEAVE_GROUP = 1
SCORE_SLOTS = 3
VMEM_LIMIT = 56 * 1024 * 1024


def _nt_dot(a, b):
    return lax.dot_general(a, b, (((1,), (1,)), ((), ())), preferred_element_type=F32)


def _dot(a, b):
    return jnp.dot(a, b, preferred_element_type=F32)


def _head_slope(h):
    return lax.bitcast_convert_type((127 - 2 * (h + 1)) << 23, F32) * LOG2_E


def _lambda(lq1, lk1, lq2, lk2, lambda_init):
    return (jnp.exp(jnp.sum(lq1[...] * lk1[...], axis=-1, keepdims=True))
            - jnp.exp(jnp.sum(lq2[...] * lk2[...], axis=-1, keepdims=True)) + lambda_init)


def _sub_ln(o0, o1, lam, g, lambda_init):
    o = o0 - lam * o1
    ms = jnp.mean(o * o, axis=-1, keepdims=True)
    return (o * lax.rsqrt(ms + SUBLN_EPS) * g) * (1.0 - lambda_init)


def _proj_pool_kernel(x_ref, g_ref, w_in_ref, w_pool_ref, pscale_ref, buf_ref,
                      k_ref, v_ref, qb_ref, kb_ref, vb_ref, pool_ref, tail_ref, ext_ref,
                      *, pos0, feature_major):
    bb, tm, _ = x_ref.shape
    rows = bb * tm
    t = pl.program_id(1)

    @pl.when(t == 0)
    def _():
        ext_ref[:, 0:TAIL, :] = buf_ref[...]

    x = x_ref[...].reshape(rows, D_MODEL)
    ms = jnp.mean(x * x, axis=-1, keepdims=True)
    hn = (x * lax.rsqrt(ms + EPS) * g_ref[...]).astype(BF16)

    def proj(j):
        return _dot(hn, w_in_ref[:, j * 512:(j + 1) * 512].astype(BF16))

    ext_ref[:, TAIL:TAIL + tm, :] = proj(0).reshape(bb, tm, POOL_WIDTH)
    pos = pos0 + t * tm + lax.broadcasted_iota(jnp.int32, (1, tm, 1), 1)

    def pool_group(g):
        w = POOL_WINDOWS[g]
        lanes = slice(g * POOL_GROUP, (g + 1) * POOL_GROUP)
        ext = ext_ref[:, :, lanes]
        acc, span = ext, 1
        while span < w:
            acc = acc + pltpu.roll(acc, span, axis=1)
            span *= 2
        cnt = jnp.minimum(pos + 1, w).astype(F32)
        diff = (acc[:, TAIL:, :] / cnt - ext[:, TAIL:, :]).reshape(rows, POOL_GROUP).astype(BF16)
        y = _dot(diff, w_pool_ref[g].astype(BF16)) * pscale_ref[:, lanes]
        pool_ref[:, :, lanes] = y.astype(BF16).reshape(bb, tm, POOL_GROUP)

    k = proj(2)
    kb_ref[...] = k.astype(BF16).reshape(bb, tm, QK_WIDTH)
    if feature_major:
        assert bb == 1
        k_ref[0] = k.T
    else:
        k_ref[...] = k.reshape(bb, tm, QK_WIDTH)
    pool_group(0)
    pool_group(1)
    v = proj(3)
    if feature_major:
        vt = v.T.astype(BF16)
        for h in range(N_HEADS):
            vb_ref[0, h * VT_ROWS:h * VT_ROWS + V_DIM, :] = vt[h * V_DIM:(h + 1) * V_DIM]
            vb_ref[0, h * VT_ROWS + V_DIM:(h + 1) * VT_ROWS, :] = jnp.ones((SUM_ROWS, tm), BF16)
    else:
        vb_ref[...] = v.astype(BF16).reshape(bb, tm, V_WIDTH)
    for h in range(N_HEADS):
        v_ref[:, pl.ds(h, tm, stride=N_HEADS), :] = v[:, h * V_DIM:(h + 1) * V_DIM].reshape(bb, tm, V_DIM)
    pool_group(2)
    pool_group(3)
    q = proj(1)
    qb_ref[...] = (q * (LOG2_E / math.sqrt(HEAD_DIM))).astype(BF16).reshape(bb, tm, QK_WIDTH)

    tail = ext_ref[:, tm:tm + TAIL, :]
    tail_ref[...] = tail
    ext_ref[:, 0:TAIL, :] = tail


def _proj_pool(x, buf, g, w_in, w_pool, pscale, *, pos0, bb, tm, feature_major):
    B, T, _ = x.shape
    grid = (B // bb, T // tm)
    row_spec = lambda width: pl.BlockSpec((bb, tm, width), lambda b, t: (b, t, 0))
    const = lambda shape: pl.BlockSpec(shape, lambda b, t: (0,) * len(shape), pipeline_mode=pl.Buffered(1))
    seq_spec = pl.BlockSpec((bb, TAIL, POOL_WIDTH), lambda b, t: (b, 0, 0))
    act = lambda dtype: jax.ShapeDtypeStruct((B, T, QK_WIDTH), dtype)
    if feature_major:
        fm_spec = lambda rows: pl.BlockSpec((bb, rows, tm), lambda b, t: (b, 0, t))
        fm_shape = lambda rows, dtype: jax.ShapeDtypeStruct((B, rows, T), dtype)
        k_spec, k_shape = fm_spec(QK_WIDTH), fm_shape(QK_WIDTH, F32)
        vb_spec, vb_shape = fm_spec(N_HEADS * VT_ROWS), fm_shape(N_HEADS * VT_ROWS, BF16)
    else:
        k_spec, k_shape = row_spec(QK_WIDTH), act(F32)
        vb_spec, vb_shape = row_spec(V_WIDTH), act(BF16)
    return pl.pallas_call(
        functools.partial(_proj_pool_kernel, pos0=pos0, feature_major=feature_major),
        grid=grid,
        in_specs=[row_spec(D_MODEL), const((1, D_MODEL)), const((D_MODEL, 4 * 512)),
                  const((len(POOL_WINDOWS), POOL_GROUP, POOL_GROUP)), const((1, POOL_WIDTH)), seq_spec],
        out_specs=[k_spec, pl.BlockSpec((bb, tm * N_HEADS, V_DIM), lambda b, t: (b, t, 0)),
                   row_spec(QK_WIDTH), row_spec(QK_WIDTH), vb_spec, row_spec(POOL_WIDTH), seq_spec],
        out_shape=[k_shape, jax.ShapeDtypeStruct((B, T * N_HEADS, V_DIM), F32),
                   act(BF16), act(BF16), vb_shape, act(BF16),
                   jax.ShapeDtypeStruct((B, TAIL, POOL_WIDTH), F32)],
        scratch_shapes=[pltpu.VMEM((bb, TAIL + tm, POOL_WIDTH), F32)],
        compiler_params=pltpu.CompilerParams(dimension_semantics=("arbitrary", "arbitrary"),
                                             vmem_limit_bytes=VMEM_LIMIT),
        name="proj_pool",
    )(x, g, w_in, w_pool, pscale, buf)


def _prompt_attn_kernel(lq1, lk1, lq2, lk2, gcol_ref, q_ref, k_ref, vt_ref, o_ref,
                        key_pos_ref, bias_diag_ref, *s_slot, lambda_init, blk):
    n_blocks = q_ref.shape[1] // blk
    h = pl.program_id(0)
    slope = _head_slope(jnp.full((1, 1), h, jnp.int32))

    @pl.when((h == 0) & (pl.program_id(1) == 0))
    def _():
        kg = lax.broadcasted_iota(jnp.int32, key_pos_ref.shape, 0)
        lane = lax.broadcasted_iota(jnp.int32, key_pos_ref.shape, 1)
        code = jnp.where(lane < BIAS_TERMS, kg >> POS_LOW_BITS,
                         jnp.where(lane < 2 * BIAS_TERMS, kg & ((1 << POS_LOW_BITS) - 1),
                                   jnp.where(lane < 3 * BIAS_TERMS, 1, 0)))
        key_pos_ref[...] = code.astype(F32).astype(BF16)

    @pl.when(pl.program_id(1) == 0)
    def _():
        kk = lax.broadcasted_iota(jnp.int32, bias_diag_ref.shape, 0)
        qq = lax.broadcasted_iota(jnp.int32, bias_diag_ref.shape, 1) & (blk - 1)
        visible = (kk >> CHUNK_BITS) <= (qq >> CHUNK_BITS)
        bias_diag_ref[...] = jnp.where(visible, -slope * jnp.abs(qq - kk).astype(F32), -jnp.inf)

    lam = _lambda(lq1, lk1, lq2, lk2, lambda_init)
    feat = lax.broadcasted_iota(jnp.int32, (HEAD_LANES, blk), 0)
    zero = jnp.zeros((HEAD_LANES, blk), BF16)
    q_off = (lax.broadcasted_iota(jnp.int32, (1, 2 * blk), 1) & (blk - 1)).astype(F32)
    bias_row = lax.broadcasted_iota(jnp.int32, (BIAS_ROWS, 2 * blk), 0)

    def pieces(x):
        out = []
        for _ in range(BIAS_TERMS):
            piece = x.astype(BF16).astype(F32)
            out.append(piece)
            x = x - piece
        return out

    n_slots = len(s_slot)
    run_time_zero = jnp.minimum(pl.program_id(1), 0)

    def sublane_max(s):
        return jnp.max(s.reshape(blk // 8, 8, s.shape[1]), axis=0)

    def scores(c, col_max):
        n = c * blk
        qt = q_ref[0, n:n + blk, :].astype(F32).T.astype(BF16)
        q_cols = jnp.concatenate([jnp.where(feat < HEAD_DIM, qt, zero), jnp.where(feat >= HEAD_DIM, qt, zero)],
                                 axis=1)
        s_diag = _dot(k_ref[0, n:n + blk, :], q_cols) + bias_diag_ref[...]
        s_slot[c % n_slots][n:n + blk, :] = s_diag
        m = sublane_max(s_diag)
        yield
        if c > 0:
            terms = (pieces(slope * float(1 << POS_LOW_BITS)) + pieces(slope)
                     + pieces(-slope * (q_off + float(n))))
            bias_cols = jnp.zeros(bias_row.shape, F32)
            for r, term in enumerate(terms):
                bias_cols = jnp.where(bias_row == r, term, bias_cols)
            w = jnp.concatenate([q_cols, bias_cols.astype(BF16),
                                 jnp.zeros((HEAD_LANES - BIAS_ROWS, 2 * blk), BF16)], axis=0)
            for j in range(c):
                rows = slice(j * blk, (j + 1) * blk)
                s_past = _dot(jnp.concatenate([k_ref[0, rows, :], key_pos_ref[rows, :]], axis=1), w)
                s_slot[c % n_slots][rows, :] = s_past
                m = jnp.maximum(m, sublane_max(s_past))
                yield
        col_max[c] = jnp.max(m, axis=0, keepdims=True)

    def outputs(c, m):
        n = c * blk
        acc = None
        for j in range(c + 1):
            rows = slice(j * blk, (j + 1) * blk)
            start = pl.multiple_of(run_time_zero + j * blk, blk)
            if j < c:
                p = jnp.exp2(s_slot[c % n_slots][pl.ds(start, blk), :] - m)
            else:
                hb = blk // 2
                assert hb % (1 << CHUNK_BITS) == 0
                top = jnp.exp2(s_slot[c % n_slots][pl.ds(start, hb), :] - m)
                bottom = []
                for half in range(2):
                    cols = slice(half * blk + hb, (half + 1) * blk)
                    bottom += [jnp.zeros((hb, hb), F32),
                               jnp.exp2(s_slot[c % n_slots][pl.ds(start + hb, hb), cols] - m[:, cols])]
                p = jnp.concatenate([top, jnp.concatenate(bottom, axis=1)], axis=0)
            pv = _dot(vt_ref[0, :, rows], p.astype(BF16))
            acc = pv if acc is None else acc + pv
            yield
        o = [acc[0:V_DIM, half * blk:(half + 1) * blk] / acc[V_DIM:V_DIM + 1, half * blk:(half + 1) * blk]
             for half in range(2)]
        ot = o[0] - lam * o[1]
        ms = jnp.mean(ot * ot, axis=0, keepdims=True)
        yt = (ot * lax.rsqrt(ms + SUBLN_EPS) * gcol_ref[...]) * (1.0 - lambda_init)
        o_ref[0, n:n + blk, :] = yt.T.astype(o_ref.dtype)

    col_max = {}
    ahead = SCORE_SLOTS - 1
    for c in range(n_blocks - 1, max(n_blocks - 1 - ahead, -1), -1):
        for _ in scores(c, col_max):
            pass
    for c in reversed(range(n_blocks)):
        gens = [outputs(c, col_max[c])]
        if c - ahead >= 0:
            gens.insert(0, scores(c - ahead, col_max))
        while gens:
            for g in list(gens):
                for _ in range(INTERLEAVE_GROUP):
                    if next(g, StopIteration) is StopIteration:
                        gens.remove(g)
                        break


def _prompt_attn(qb, kb, vtb, lams, subln_gcol, *, lambda_init, blk):
    B, T, _ = qb.shape
    grid = (N_HEADS, B)
    vec = lambda n: pl.BlockSpec((1, n), lambda h, b: (0, 0))
    qk_spec = pl.BlockSpec((1, T, HEAD_LANES), lambda h, b: (b, 0, h))
    vt_spec = pl.BlockSpec((1, VT_ROWS, T), lambda h, b: (b, h, 0))
    return pl.pallas_call(
        functools.partial(_prompt_attn_kernel, lambda_init=lambda_init, blk=blk),
        grid=grid,
        in_specs=[vec(HEAD_DIM)] * 4 + [pl.BlockSpec((V_DIM, 1), lambda h, b: (0, 0)), qk_spec, qk_spec, vt_spec],
        out_specs=qk_spec,
        out_shape=jax.ShapeDtypeStruct((B, T, V_WIDTH), BF16),
        scratch_shapes=[pltpu.VMEM((T - blk, HEAD_LANES), BF16), pltpu.VMEM((blk, 2 * blk), F32)]
                       + [pltpu.VMEM((T, 2 * blk), F32)] * SCORE_SLOTS,
        compiler_params=pltpu.CompilerParams(dimension_semantics=("arbitrary",) * 2,
                                             vmem_limit_bytes=VMEM_LIMIT),
        name="prompt_attn",
    )(*lams, subln_gcol, qb, kb, vtb)


def _sample_attn_step(j, nj, lq1, lk1, lq2, lk2, g_ref, q_ref, kn_ref, vn_ref, ckt_ref, cv_ref, o_ref,
                      qrows_ref, m_ref, l_ref, acc_ref, *, lambda_init, past_len):
    tq = q_ref.shape[1]
    head_rows = 2 * tq
    n_rows = N_HEADS * head_rows
    kb = ckt_ref.shape[2]

    tq_bits = tq.bit_length() - 1
    assert tq == 1 << tq_bits
    row = lax.broadcasted_iota(jnp.int32, (n_rows, 1), 0)
    q_rel = row & (tq - 1)
    slope = _head_slope(row >> (tq_bits + 1))

    @pl.when(j == 0)
    def _():
        q = q_ref[0]
        tiled = jnp.concatenate([q] * (2 * N_HEADS), axis=0)
        r = lax.broadcasted_iota(jnp.int32, tiled.shape, 0)
        c = lax.broadcasted_iota(jnp.int32, tiled.shape, 1)
        qrows_ref[...] = jnp.where((r >> tq_bits) == (c >> HEAD_DIM_BITS), tiled, jnp.zeros_like(tiled))
        m_ref[...] = jnp.full(m_ref.shape, -jnp.inf, F32)
        l_ref[...] = jnp.zeros(l_ref.shape, F32)
        acc_ref[...] = jnp.zeros(acc_ref.shape, F32)

    def update(s, values):
        m_old = m_ref[...]
        m_new = jnp.maximum(m_old, jnp.max(s, axis=-1, keepdims=True))
        p = jnp.exp2(s - m_new)
        alpha = jnp.exp2(m_old - m_new)
        l_ref[...] = alpha * l_ref[...] + jnp.sum(p, axis=-1, keepdims=True)
        m_ref[...] = m_new
        acc_ref[...] = alpha * acc_ref[...] + _dot(p.astype(BF16), values)

    scores = []

    def score_cache():
        k_rel = (j * kb - past_len + lax.broadcasted_iota(jnp.int32, (1, kb), 1)).astype(F32)
        scores.append(_dot(qrows_ref[...], ckt_ref[0].astype(BF16)) + slope * k_rel)

    def attend_cache():
        values = jnp.concatenate(
            [cv_ref[0, pl.ds(h, kb, stride=N_HEADS), :].astype(BF16) for h in range(N_HEADS)], axis=1)
        update(scores.pop(), values)

    def finish_stream():
        @pl.when(j == nj - 1)
        def _():
            k_new = lax.broadcasted_iota(jnp.int32, (1, tq), 1)
            update(_nt_dot(qrows_ref[...], kn_ref[0]) + slope * (q_rel - jnp.abs(q_rel - k_new)).astype(F32),
                   vn_ref[0])
            lam = _lambda(lq1, lk1, lq2, lk2, lambda_init)
            out = acc_ref[...] / l_ref[...]
            for h in range(N_HEADS):
                own = out[h * head_rows:(h + 1) * head_rows, h * V_DIM:(h + 1) * V_DIM]
                o_ref[0, :, h * V_DIM:(h + 1) * V_DIM] = _sub_ln(own[:tq], own[tq:], lam, g_ref[...],
                                                                 lambda_init).astype(o_ref.dtype)

    return score_cache, attend_cache, finish_stream


def _out_mlp_kernel(pool_ref, attn_ref, x_ref, w_out_ref, gf_ref, w_up_ref, w_down_ref, gl_ref, y_ref):
    _out_mlp_tile(pool_ref, attn_ref, x_ref, w_out_ref, gf_ref, w_up_ref, w_down_ref, gl_ref, y_ref)


def _out_mlp_tile(pool_ref, attn_ref, x_ref, w_out_ref, gf_ref, w_up_ref, w_down_ref, gl_ref, y_ref,
                  after_chunk=()):
    mixed = (_dot(pool_ref[...], w_out_ref[0:POOL_WIDTH, :].astype(BF16))
             + _dot(attn_ref[...], w_out_ref[POOL_WIDTH:, :].astype(BF16)))
    h = x_ref[...] + mixed
    ms = jnp.mean(h * h, axis=-1, keepdims=True)
    hn = (h * lax.rsqrt(ms + EPS) * gf_ref[...]).astype(BF16)
    y = h
    for c in range(D_FF // FF_CHUNK):
        cols = slice(c * FF_CHUNK, (c + 1) * FF_CHUNK)
        a = jnp.maximum(_dot(hn, w_up_ref[:, cols]), 0.0)
        y = y + _dot((a * a).astype(BF16), w_down_ref[cols, :])
        if c < len(after_chunk):
            after_chunk[c]()
    ms = jnp.mean(y * y, axis=-1, keepdims=True)
    y_ref[...] = y * lax.rsqrt(ms + EPS) * gl_ref[...]


def _out_mlp(pool, attn, x, w_out, gf, w_up, w_down, gl, *, tm):
    rows = x.shape[0]
    row_spec = lambda width: pl.BlockSpec((tm, width), lambda r: (r, 0))
    const = lambda shape: pl.BlockSpec(shape, lambda r: (0, 0), pipeline_mode=pl.Buffered(1))
    return pl.pallas_call(
        _out_mlp_kernel,
        grid=(rows // tm,),
        in_specs=[row_spec(POOL_WIDTH), row_spec(V_WIDTH), row_spec(D_MODEL),
                  const((D_MODEL, D_MODEL)), const((1, D_MODEL)), const((D_MODEL, D_FF)),
                  const((D_FF, D_MODEL)), const((1, D_MODEL))],
        out_specs=row_spec(D_MODEL),
        out_shape=jax.ShapeDtypeStruct((rows, D_MODEL), F32),
        compiler_params=pltpu.CompilerParams(dimension_semantics=("arbitrary",),
                                             vmem_limit_bytes=VMEM_LIMIT),
        name="out_mlp",
    )(pool, attn, x, w_out, gf, w_up, w_down, gl)


def _out_mlp_sample_attn_kernel(*refs, parts, lambda_init, past_len):
    mlp_in, samp_in, (y_ref, o_ref), scratch = refs[:8], refs[8:18], refs[18:20], refs[20:]
    r = pl.program_id(0)
    score_cache, attend_cache, finish_stream = _sample_attn_step(
        r % parts, parts, *samp_in, o_ref, *scratch, lambda_init=lambda_init, past_len=past_len)
    _out_mlp_tile(*mlp_in, y_ref, after_chunk=(score_cache, attend_cache))
    finish_stream()


def _out_mlp_with_sample_attn(pool, attn, x, w_out, gf, w_up, w_down, gl,
                              qb, kb_new, vb_new, cache_kt, cache_v, lams, subln_g, *, tm, lambda_init):
    rows = x.shape[0]
    S, tq, _ = qb.shape
    past_len = cache_kt.shape[2]
    steps = rows // tm
    assert steps % S == 0, "every stream gets the same number of grid steps"
    parts = steps // S
    kblk = past_len // parts
    n_rows = 2 * N_HEADS * tq
    row_spec = lambda width: pl.BlockSpec((tm, width), lambda r: (r, 0))
    const = lambda shape: pl.BlockSpec(shape, lambda r: (0, 0), pipeline_mode=pl.Buffered(1))
    vec = lambda n: pl.BlockSpec((1, n), lambda r: (0, 0))
    new_spec = pl.BlockSpec((1, tq, QK_WIDTH), lambda r: (r // parts, 0, 0))
    kt_spec = pl.BlockSpec((1, QK_WIDTH, kblk), lambda r: (r // parts, 0, r % parts))
    v_spec = pl.BlockSpec((1, kblk * N_HEADS, V_DIM), lambda r: (r // parts, r % parts, 0))
    return pl.pallas_call(
        functools.partial(_out_mlp_sample_attn_kernel, parts=parts, lambda_init=lambda_init, past_len=past_len),
        grid=(steps,),
        in_specs=[row_spec(POOL_WIDTH), row_spec(V_WIDTH), row_spec(D_MODEL),
                  const((D_MODEL, D_MODEL)), const((1, D_MODEL)), const((D_MODEL, D_FF)),
                  const((D_FF, D_MODEL)), const((1, D_MODEL))]
                 + [vec(HEAD_DIM)] * 4 + [vec(V_DIM), new_spec, new_spec, new_spec, kt_spec, v_spec],
        out_specs=[row_spec(D_MODEL), new_spec],
        out_shape=[jax.ShapeDtypeStruct((rows, D_MODEL), F32), jax.ShapeDtypeStruct((S, tq, V_WIDTH), BF16)],
        scratch_shapes=[pltpu.VMEM((n_rows, QK_WIDTH), BF16), pltpu.VMEM((n_rows, 1), F32),
                        pltpu.VMEM((n_rows, 1), F32), pltpu.VMEM((n_rows, V_WIDTH), F32)],
        compiler_params=pltpu.CompilerParams(dimension_semantics=("arbitrary",),
                                             vmem_limit_bytes=VMEM_LIMIT),
        name="out_mlp_sample_attn",
    )(pool, attn, x, w_out, gf, w_up, w_down, gl, *lams, subln_g, qb, kb_new, vb_new, cache_kt, cache_v)


def _keys_from_feature_major(kt, frames):
    streams = kt.shape[0]
    return jnp.transpose(kt.reshape(streams, N_HEADS, 2, HEAD_DIM, frames), (0, 4, 1, 2, 3))[None]


def kernel(x_prompt, x_sample, state_pool, cache_k, cache_v, norm_mix_g, w_in, w_pool, pool_scale,
           lambda_q1, lambda_k1, lambda_q2, lambda_k2, subln_g, w_out, norm_ffn_g, w_up, w_down,
           norm_final_g):
    assert w_in.shape[0] == 1, "one layer per call"
    B, T, _ = x_prompt.shape
    S, TS, _ = x_sample.shape
    past_len = cache_k.shape[2]
    lambda_init = 0.8 - 0.6 * math.exp(-0.3 * 0)

    g_mix = norm_mix_g[0][None]
    g_ffn = norm_ffn_g[0][None]
    g_fin = norm_final_g[None]
    g_sub = subln_g[0][None]
    pscale = pool_scale[0][None]
    lams = (lambda_q1[0][None], lambda_k1[0][None], lambda_q2[0][None], lambda_k2[0][None])
    w_in_f, w_pool_f, w_out_f = w_in[0], w_pool[0], w_out[0]
    w_up_b = w_up[0].astype(BF16)
    w_down_b = w_down[0].astype(BF16)

    zero_buf = jnp.zeros((B, TAIL, POOL_WIDTH), F32)
    kt_p, v_p, qb, kb, vtb, pool_p, tail_p = _proj_pool(
        x_prompt, zero_buf, g_mix, w_in_f, w_pool_f, pscale, pos0=0, bb=1, tm=PROJ_ROWS,
        feature_major=True)
    attn_p = _prompt_attn(qb, kb, vtb, lams, g_sub.reshape(V_DIM, 1), lambda_init=lambda_init, blk=ATTN_BLOCK)

    buf_s = jnp.pad(state_pool[0], ((0, 0), (TAIL - POOL_STATE, 0), (0, 0)))
    k_s, v_s, qb_s, kb_s, vb_s, pool_s, tail_s = _proj_pool(
        x_sample, buf_s, g_mix, w_in_f, w_pool_f, pscale, pos0=past_len, bb=S, tm=TS,
        feature_major=False)
    cache_kt = jnp.transpose(cache_k[0], (0, 2, 3, 4, 1)).reshape(S, QK_WIDTH, past_len)
    cache_vi = cache_v[0].reshape(S, past_len * N_HEADS, V_DIM)

    y_p, attn_s = _out_mlp_with_sample_attn(
        pool_p.reshape(B * T, -1), attn_p.reshape(B * T, -1), x_prompt.reshape(B * T, -1),
        w_out_f, g_ffn, w_up_b, w_down_b, g_fin, qb_s, kb_s, vb_s, cache_kt, cache_vi, lams, g_sub,
        tm=PROMPT_ROWS, lambda_init=lambda_init)
    y_p = y_p.reshape(B, T, D_MODEL)
    y_s = _out_mlp(pool_s.reshape(S * TS, -1), attn_s.reshape(S * TS, -1), x_sample.reshape(S * TS, -1),
                   w_out_f, g_ffn, w_up_b, w_down_b, g_fin, tm=S * TS).reshape(S, TS, D_MODEL)

    return (y_p, y_s,
            tail_p[:, TAIL - POOL_STATE:][None],
            _keys_from_feature_major(kt_p, T),
            v_p.reshape(1, B, T, N_HEADS, V_DIM),
            tail_s[:, TAIL - POOL_STATE:][None],
            k_s.reshape(1, S, TS, N_HEADS, 2, HEAD_DIM),
            v_s.reshape(1, S, TS, N_HEADS, V_DIM))
```

```python
import functools
import math

import jax
import jax.numpy as jnp
from jax import lax
from jax.experimental import pallas as pl
from jax.experimental.pallas import tpu as pltpu

D_MODEL = 1024
POOL_WIDTH = 512
POOL_WINDOWS = (2, 4, 8, 16)
POOL_GROUP = 128
POOL_STATE = 15
N_HEADS = 4
HEAD_DIM = 64
V_DIM = 128
QK_WIDTH = 512
V_WIDTH = N_HEADS * V_DIM
D_FF = 4096
CHUNK_BITS = 6
HEAD_DIM_BITS = 6
assert HEAD_DIM == 1 << HEAD_DIM_BITS
EPS = 1e-6
SUBLN_EPS = 1e-5

TAIL = 16
HEAD_LANES = 2 * HEAD_DIM
SUM_ROWS = 16
VT_ROWS = V_DIM + SUM_ROWS
BIAS_TERMS = 3
BIAS_ROWS = 16
assert 3 * BIAS_TERMS <= BIAS_ROWS
POS_LOW_BITS = 8

BF16 = jnp.bfloat16
F32 = jnp.float32
LOG2_E = math.log2(math.e)

PROJ_ROWS = 1024
PROMPT_ROWS = 512
ATTN_BLOCK = 256
FF_CHUNK = 1024
INTERLEAVE_GROUP = 1
SCORE_SLOTS = 3
VMEM_LIMIT = 56 * 1024 * 1024


def _nt_dot(a, b):
    return lax.dot_general(a, b, (((1,), (1,)), ((), ())), preferred_element_type=F32)


def _dot(a, b):
    return jnp.dot(a, b, preferred_element_type=F32)


def _head_slope(h):
    return lax.bitcast_convert_type((127 - 2 * (h + 1)) << 23, F32) * LOG2_E


def _lambda(lq1, lk1, lq2, lk2, lambda_init):
    return (jnp.exp(jnp.sum(lq1[...] * lk1[...], axis=-1, keepdims=True))
            - jnp.exp(jnp.sum(lq2[...] * lk2[...], axis=-1, keepdims=True)) + lambda_init)


def _sub_ln(o0, o1, lam, g, lambda_init):
    o = o0 - lam * o1
    ms = jnp.mean(o * o, axis=-1, keepdims=True)
    return (o * lax.rsqrt(ms + SUBLN_EPS) * g) * (1.0 - lambda_init)


def _proj_pool_kernel(x_ref, g_ref, w_in_ref, w_pool_ref, pscale_ref, buf_ref,
                      k_ref, v_ref, qb_ref, kb_ref, vb_ref, pool_ref, tail_ref, ext_ref,
                      *, pos0, feature_major):
    bb, tm, _ = x_ref.shape
    rows = bb * tm
    t = pl.program_id(1)

    @pl.when(t == 0)
    def _():
        ext_ref[:, 0:TAIL, :] = buf_ref[...]

    x = x_ref[...].reshape(rows, D_MODEL)
    ms = jnp.mean(x * x, axis=-1, keepdims=True)
    hn = (x * lax.rsqrt(ms + EPS) * g_ref[...]).astype(BF16)

    def proj(j):
        return _dot(hn, w_in_ref[:, j * 512:(j + 1) * 512].astype(BF16))

    ext_ref[:, TAIL:TAIL + tm, :] = proj(0).reshape(bb, tm, POOL_WIDTH)
    pos = pos0 + t * tm + lax.broadcasted_iota(jnp.int32, (1, tm, 1), 1)

    def pool_group(g):
        w = POOL_WINDOWS[g]
        lanes = slice(g * POOL_GROUP, (g + 1) * POOL_GROUP)
        ext = ext_ref[:, :, lanes]
        acc, span = ext, 1
        while span < w:
            acc = acc + pltpu.roll(acc, span, axis=1)
            span *= 2
        cnt = jnp.minimum(pos + 1, w).astype(F32)
        diff = (acc[:, TAIL:, :] / cnt - ext[:, TAIL:, :]).reshape(rows, POOL_GROUP).astype(BF16)
        y = _dot(diff, w_pool_ref[g].astype(BF16)) * pscale_ref[:, lanes]
        pool_ref[:, :, lanes] = y.astype(BF16).reshape(bb, tm, POOL_GROUP)

    k = proj(2)
    kb_ref[...] = k.astype(BF16).reshape(bb, tm, QK_WIDTH)
    if feature_major:
        assert bb == 1
        k_ref[0] = k.T
    else:
        k_ref[...] = k.reshape(bb, tm, QK_WIDTH)
    pool_group(0)
    pool_group(1)
    v = proj(3)
    if feature_major:
        vt = v.T.astype(BF16)
        for h in range(N_HEADS):
            vb_ref[0, h * VT_ROWS:h * VT_ROWS + V_DIM, :] = vt[h * V_DIM:(h + 1) * V_DIM]
            vb_ref[0, h * VT_ROWS + V_DIM:(h + 1) * VT_ROWS, :] = jnp.ones((SUM_ROWS, tm), BF16)
    else:
        vb_ref[...] = v.astype(BF16).reshape(bb, tm, V_WIDTH)
    for h in range(N_HEADS):
        v_ref[:, pl.ds(h, tm, stride=N_HEADS), :] = v[:, h * V_DIM:(h + 1) * V_DIM].reshape(bb, tm, V_DIM)
    pool_group(2)
    pool_group(3)
    q = proj(1)
    qb_ref[...] = (q * (LOG2_E / math.sqrt(HEAD_DIM))).astype(BF16).reshape(bb, tm, QK_WIDTH)

    tail = ext_ref[:, tm:tm + TAIL, :]
    tail_ref[...] = tail
    ext_ref[:, 0:TAIL, :] = tail


def _proj_pool(x, buf, g, w_in, w_pool, pscale, *, pos0, bb, tm, feature_major):
    B, T, _ = x.shape
    grid = (B // bb, T // tm)
    row_spec = lambda width: pl.BlockSpec((bb, tm, width), lambda b, t: (b, t, 0))
    const = lambda shape: pl.BlockSpec(shape, lambda b, t: (0,) * len(shape), pipeline_mode=pl.Buffered(1))
    seq_spec = pl.BlockSpec((bb, TAIL, POOL_WIDTH), lambda b, t: (b, 0, 0))
    act = lambda dtype: jax.ShapeDtypeStruct((B, T, QK_WIDTH), dtype)
    if feature_major:
        fm_spec = lambda rows: pl.BlockSpec((bb, rows, tm), lambda b, t: (b, 0, t))
        fm_shape = lambda rows, dtype: jax.ShapeDtypeStruct((B, rows, T), dtype)
        k_spec, k_shape = fm_spec(QK_WIDTH), fm_shape(QK_WIDTH, F32)
        vb_spec, vb_shape = fm_spec(N_HEADS * VT_ROWS), fm_shape(N_HEADS * VT_ROWS, BF16)
    else:
        k_spec, k_shape = row_spec(QK_WIDTH), act(F32)
        vb_spec, vb_shape = row_spec(V_WIDTH), act(BF16)
    return pl.pallas_call(
        functools.partial(_proj_pool_kernel, pos0=pos0, feature_major=feature_major),
        grid=grid,
        in_specs=[row_spec(D_MODEL), const((1, D_MODEL)), const((D_MODEL, 4 * 512)),
                  const((len(POOL_WINDOWS), POOL_GROUP, POOL_GROUP)), const((1, POOL_WIDTH)), seq_spec],
        out_specs=[k_spec, pl.BlockSpec((bb, tm * N_HEADS, V_DIM), lambda b, t: (b, t, 0)),
                   row_spec(QK_WIDTH), row_spec(QK_WIDTH), vb_spec, row_spec(POOL_WIDTH), seq_spec],
        out_shape=[k_shape, jax.ShapeDtypeStruct((B, T * N_HEADS, V_DIM), F32),
                   act(BF16), act(BF16), vb_shape, act(BF16),
                   jax.ShapeDtypeStruct((B, TAIL, POOL_WIDTH), F32)],
        scratch_shapes=[pltpu.VMEM((bb, TAIL + tm, POOL_WIDTH), F32)],
        compiler_params=pltpu.CompilerParams(dimension_semantics=("arbitrary", "arbitrary"),
                                             vmem_limit_bytes=VMEM_LIMIT),
        name="proj_pool",
    )(x, g, w_in, w_pool, pscale, buf)


def _prompt_attn_kernel(lq1, lk1, lq2, lk2, gcol_ref, q_ref, k_ref, vt_ref, o_ref,
                        key_pos_ref, bias_diag_ref, *s_slot, lambda_init, blk):
    n_blocks = q_ref.shape[1] // blk
    h = pl.program_id(0)
    slope = _head_slope(jnp.full((1, 1), h, jnp.int32))

    @pl.when((h == 0) & (pl.program_id(1) == 0))
    def _():
        kg = lax.broadcasted_iota(jnp.int32, key_pos_ref.shape, 0)
        lane = lax.broadcasted_iota(jnp.int32, key_pos_ref.shape, 1)
        code = jnp.where(lane < BIAS_TERMS, kg >> POS_LOW_BITS,
                         jnp.where(lane < 2 * BIAS_TERMS, kg & ((1 << POS_LOW_BITS) - 1),
                                   jnp.where(lane < 3 * BIAS_TERMS, 1, 0)))
        key_pos_ref[...] = code.astype(F32).astype(BF16)

    @pl.when(pl.program_id(1) == 0)
    def _():
        kk = lax.broadcasted_iota(jnp.int32, bias_diag_ref.shape, 0)
        qq = lax.broadcasted_iota(jnp.int32, bias_diag_ref.shape, 1) & (blk - 1)
        visible = (kk >> CHUNK_BITS) <= (qq >> CHUNK_BITS)
        bias_diag_ref[...] = jnp.where(visible, -slope * jnp.abs(qq - kk).astype(F32), -jnp.inf)

    lam = _lambda(lq1, lk1, lq2, lk2, lambda_init)
    feat = lax.broadcasted_iota(jnp.int32, (HEAD_LANES, blk), 0)
    zero = jnp.zeros((HEAD_LANES, blk), BF16)
    q_off = (lax.broadcasted_iota(jnp.int32, (1, 2 * blk), 1) & (blk - 1)).astype(F32)
    bias_row = lax.broadcasted_iota(jnp.int32, (BIAS_ROWS, 2 * blk), 0)

    def pieces(x):
        out = []
        for _ in range(BIAS_TERMS):
            piece = x.astype(BF16).astype(F32)
            out.append(piece)
            x = x - piece
        return out

    n_slots = len(s_slot)
    run_time_zero = jnp.minimum(pl.program_id(1), 0)

    def sublane_max(s):
        return jnp.max(s.reshape(blk // 8, 8, s.shape[1]), axis=0)

    def scores(c, col_max):
        n = c * blk
        qt = q_ref[0, n:n + blk, :].astype(F32).T.astype(BF16)
        q_cols = jnp.concatenate([jnp.where(feat < HEAD_DIM, qt, zero), jnp.where(feat >= HEAD_DIM, qt, zero)],
                                 axis=1)
        s_diag = _dot(k_ref[0, n:n + blk, :], q_cols) + bias_diag_ref[...]
        s_slot[c % n_slots][n:n + blk, :] = s_diag
        m = sublane_max(s_diag)
        yield
        if c > 0:
            terms = (pieces(slope * float(1 << POS_LOW_BITS)) + pieces(slope)
                     + pieces(-slope * (q_off + float(n))))
            bias_cols = jnp.zeros(bias_row.shape, F32)
            for r, term in enumerate(terms):
                bias_cols = jnp.where(bias_row == r, term, bias_cols)
            w = jnp.concatenate([q_cols, bias_cols.astype(BF16),
                                 jnp.zeros((HEAD_LANES - BIAS_ROWS, 2 * blk), BF16)], axis=0)
            for j in range(c):
                rows = slice(j * blk, (j + 1) * blk)
                s_past = _dot(jnp.concatenate([k_ref[0, rows, :], key_pos_ref[rows, :]], axis=1), w)
                s_slot[c % n_slots][rows, :] = s_past
                m = jnp.maximum(m, sublane_max(s_past))
                yield
        col_max[c] = jnp.max(m, axis=0, keepdims=True)

    def outputs(c, m):
        n = c * blk
        acc = None
        for j in range(c + 1):
            rows = slice(j * blk, (j + 1) * blk)
            start = pl.multiple_of(run_time_zero + j * blk, blk)
            if j < c:
                p = jnp.exp2(s_slot[c % n_slots][pl.ds(start, blk), :] - m)
            else:
                hb = blk // 2
                assert hb % (1 << CHUNK_BITS) == 0
                top = jnp.exp2(s_slot[c % n_slots][pl.ds(start, hb), :] - m)
                bottom = []
                for half in range(2):
                    cols = slice(half * blk + hb, (half + 1) * blk)
                    bottom += [jnp.zeros((hb, hb), F32),
                               jnp.exp2(s_slot[c % n_slots][pl.ds(start + hb, hb), cols] - m[:, cols])]
                p = jnp.concatenate([top, jnp.concatenate(bottom, axis=1)], axis=0)
            pv = _dot(vt_ref[0, :, rows], p.astype(BF16))
            acc = pv if acc is None else acc + pv
            yield
        o = [acc[0:V_DIM, half * blk:(half + 1) * blk] / acc[V_DIM:V_DIM + 1, half * blk:(half + 1) * blk]
             for half in range(2)]
        ot = o[0] - lam * o[1]
        ms = jnp.mean(ot * ot, axis=0, keepdims=True)
        yt = (ot * lax.rsqrt(ms + SUBLN_EPS) * gcol_ref[...]) * (1.0 - lambda_init)
        o_ref[0, n:n + blk, :] = yt.T.astype(o_ref.dtype)

    col_max = {}
    ahead = SCORE_SLOTS - 1
    for c in range(n_blocks - 1, max(n_blocks - 1 - ahead, -1), -1):
        for _ in scores(c, col_max):
            pass
    for c in reversed(range(n_blocks)):
        gens = [outputs(c, col_max[c])]
        if c - ahead >= 0:
            gens.insert(0, scores(c - ahead, col_max))
        while gens:
            for g in list(gens):
                for _ in range(INTERLEAVE_GROUP):
                    if next(g, StopIteration) is StopIteration:
                        gens.remove(g)
                        break


def _prompt_attn(qb, kb, vtb, lams, subln_gcol, *, lambda_init, blk):
    B, T, _ = qb.shape
    grid = (N_HEADS, B)
    vec = lambda n: pl.BlockSpec((1, n), lambda h, b: (0, 0))
    qk_spec = pl.BlockSpec((1, T, HEAD_LANES), lambda h, b: (b, 0, h))
    vt_spec = pl.BlockSpec((1, VT_ROWS, T), lambda h, b: (b, h, 0))
    return pl.pallas_call(
        functools.partial(_prompt_attn_kernel, lambda_init=lambda_init, blk=blk),
        grid=grid,
        in_specs=[vec(HEAD_DIM)] * 4 + [pl.BlockSpec((V_DIM, 1), lambda h, b: (0, 0)), qk_spec, qk_spec, vt_spec],
        out_specs=qk_spec,
        out_shape=jax.ShapeDtypeStruct((B, T, V_WIDTH), BF16),
        scratch_shapes=[pltpu.VMEM((T - blk, HEAD_LANES), BF16), pltpu.VMEM((blk, 2 * blk), F32)]
                       + [pltpu.VMEM((T, 2 * blk), F32)] * SCORE_SLOTS,
        compiler_params=pltpu.CompilerParams(dimension_semantics=("arbitrary",) * 2,
                                             vmem_limit_bytes=VMEM_LIMIT),
        name="prompt_attn",
    )(*lams, subln_gcol, qb, kb, vtb)


def _sample_attn_step(j, nj, lq1, lk1, lq2, lk2, g_ref, q_ref, kn_ref, vn_ref, ckt_ref, cv_ref, o_ref,
                      qrows_ref, m_ref, l_ref, acc_ref, *, lambda_init, past_len):
    tq = q_ref.shape[1]
    head_rows = 2 * tq
    n_rows = N_HEADS * head_rows
    kb = ckt_ref.shape[2]

    tq_bits = tq.bit_length() - 1
    assert tq == 1 << tq_bits
    row = lax.broadcasted_iota(jnp.int32, (n_rows, 1), 0)
    q_rel = row & (tq - 1)
    slope = _head_slope(row >> (tq_bits + 1))

    @pl.when(j == 0)
    def _():
        q = q_ref[0]
        tiled = jnp.concatenate([q] * (2 * N_HEADS), axis=0)
        r = lax.broadcasted_iota(jnp.int32, tiled.shape, 0)
        c = lax.broadcasted_iota(jnp.int32, tiled.shape, 1)
        qrows_ref[...] = jnp.where((r >> tq_bits) == (c >> HEAD_DIM_BITS), tiled, jnp.zeros_like(tiled))
        m_ref[...] = jnp.full(m_ref.shape, -jnp.inf, F32)
        l_ref[...] = jnp.zeros(l_ref.shape, F32)
        acc_ref[...] = jnp.zeros(acc_ref.shape, F32)

    def update(s, values):
        m_old = m_ref[...]
        m_new = jnp.maximum(m_old, jnp.max(s, axis=-1, keepdims=True))
        p = jnp.exp2(s - m_new)
        alpha = jnp.exp2(m_old - m_new)
        l_ref[...] = alpha * l_ref[...] + jnp.sum(p, axis=-1, keepdims=True)
        m_ref[...] = m_new
        acc_ref[...] = alpha * acc_ref[...] + _dot(p.astype(BF16), values)

    scores = []

    def score_cache():
        k_rel = (j * kb - past_len + lax.broadcasted_iota(jnp.int32, (1, kb), 1)).astype(F32)
        scores.append(_dot(qrows_ref[...], ckt_ref[0].astype(BF16)) + slope * k_rel)

    def attend_cache():
        values = jnp.concatenate(
            [cv_ref[0, pl.ds(h, kb, stride=N_HEADS), :].astype(BF16) for h in range(N_HEADS)], axis=1)
        update(scores.pop(), values)

    def finish_stream():
        @pl.when(j == nj - 1)
        def _():
            k_new = lax.broadcasted_iota(jnp.int32, (1, tq), 1)
            update(_nt_dot(qrows_ref[...], kn_ref[0]) + slope * (q_rel - jnp.abs(q_rel - k_new)).astype(F32),
                   vn_ref[0])
            lam = _lambda(lq1, lk1, lq2, lk2, lambda_init)
            out = acc_ref[...] / l_ref[...]
            for h in range(N_HEADS):
                own = out[h * head_rows:(h + 1) * head_rows, h * V_DIM:(h + 1) * V_DIM]
                o_ref[0, :, h * V_DIM:(h + 1) * V_DIM] = _sub_ln(own[:tq], own[tq:], lam, g_ref[...],
                                                                 lambda_init).astype(o_ref.dtype)

    return score_cache, attend_cache, finish_stream


def _out_mlp_kernel(pool_ref, attn_ref, x_ref, w_out_ref, gf_ref, w_up_ref, w_down_ref, gl_ref, y_ref):
    _out_mlp_tile(pool_ref, attn_ref, x_ref, w_out_ref, gf_ref, w_up_ref, w_down_ref, gl_ref, y_ref)


def _out_mlp_tile(pool_ref, attn_ref, x_ref, w_out_ref, gf_ref, w_up_ref, w_down_ref, gl_ref, y_ref,
                  after_chunk=()):
    mixed = (_dot(pool_ref[...], w_out_ref[0:POOL_WIDTH, :].astype(BF16))
             + _dot(attn_ref[...], w_out_ref[POOL_WIDTH:, :].astype(BF16)))
    h = x_ref[...] + mixed
    y_ref[...] = h
    ms = jnp.mean(h * h, axis=-1, keepdims=True)
    hn = (h * lax.rsqrt(ms + EPS) * gf_ref[...]).astype(BF16)
    for c in range(D_FF // FF_CHUNK):
        cols = slice(c * FF_CHUNK, (c + 1) * FF_CHUNK)
        a = jnp.maximum(_dot(hn, w_up_ref[:, cols]), 0.0)
        y_ref[...] += _dot((a * a).astype(BF16), w_down_ref[cols, :])
        if c < len(after_chunk):
            after_chunk[c]()
    y = y_ref[...]
    ms = jnp.mean(y * y, axis=-1, keepdims=True)
    y_ref[...] = y * lax.rsqrt(ms + EPS) * gl_ref[...]


def _out_mlp(pool, attn, x, w_out, gf, w_up, w_down, gl, *, tm):
    rows = x.shape[0]
    row_spec = lambda width: pl.BlockSpec((tm, width), lambda r: (r, 0))
    const = lambda shape: pl.BlockSpec(shape, lambda r: (0, 0), pipeline_mode=pl.Buffered(1))
    return pl.pallas_call(
        _out_mlp_kernel,
        grid=(rows // tm,),
        in_specs=[row_spec(POOL_WIDTH), row_spec(V_WIDTH), row_spec(D_MODEL),
                  const((D_MODEL, D_MODEL)), const((1, D_MODEL)), const((D_MODEL, D_FF)),
                  const((D_FF, D_MODEL)), const((1, D_MODEL))],
        out_specs=row_spec(D_MODEL),
        out_shape=jax.ShapeDtypeStruct((rows, D_MODEL), F32),
        compiler_params=pltpu.CompilerParams(dimension_semantics=("arbitrary",),
                                             vmem_limit_bytes=VMEM_LIMIT),
        name="out_mlp",
    )(pool, attn, x, w_out, gf, w_up, w_down, gl)


def _out_mlp_sample_attn_kernel(*refs, parts, lambda_init, past_len):
    mlp_in, samp_in, (y_ref, o_ref), scratch = refs[:8], refs[8:18], refs[18:20], refs[20:]
    r = pl.program_id(0)
    score_cache, attend_cache, finish_stream = _sample_attn_step(
        r % parts, parts, *samp_in, o_ref, *scratch, lambda_init=lambda_init, past_len=past_len)
    _out_mlp_tile(*mlp_in, y_ref, after_chunk=(score_cache, attend_cache))
    finish_stream()


def _out_mlp_with_sample_attn(pool, attn, x, w_out, gf, w_up, w_down, gl,
                              qb, kb_new, vb_new, cache_kt, cache_v, lams, subln_g, *, tm, lambda_init):
    rows = x.shape[0]
    S, tq, _ = qb.shape
    past_len = cache_kt.shape[2]
    steps = rows // tm
    assert steps % S == 0, "every stream gets the same number of grid steps"
    parts = steps // S
    kblk = past_len // parts
    n_rows = 2 * N_HEADS * tq
    row_spec = lambda width: pl.BlockSpec((tm, width), lambda r: (r, 0))
    const = lambda shape: pl.BlockSpec(shape, lambda r: (0, 0), pipeline_mode=pl.Buffered(1))
    vec = lambda n: pl.BlockSpec((1, n), lambda r: (0, 0))
    new_spec = pl.BlockSpec((1, tq, QK_WIDTH), lambda r: (r // parts, 0, 0))
    kt_spec = pl.BlockSpec((1, QK_WIDTH, kblk), lambda r: (r // parts, 0, r % parts))
    v_spec = pl.BlockSpec((1, kblk * N_HEADS, V_DIM), lambda r: (r // parts, r % parts, 0))
    return pl.pallas_call(
        functools.partial(_out_mlp_sample_attn_kernel, parts=parts, lambda_init=lambda_init, past_len=past_len),
        grid=(steps,),
        in_specs=[row_spec(POOL_WIDTH), row_spec(V_WIDTH), row_spec(D_MODEL),
                  const((D_MODEL, D_MODEL)), const((1, D_MODEL)), const((D_MODEL, D_FF)),
                  const((D_FF, D_MODEL)), const((1, D_MODEL))]
                 + [vec(HEAD_DIM)] * 4 + [vec(V_DIM), new_spec, new_spec, new_spec, kt_spec, v_spec],
        out_specs=[row_spec(D_MODEL), new_spec],
        out_shape=[jax.ShapeDtypeStruct((rows, D_MODEL), F32), jax.ShapeDtypeStruct((S, tq, V_WIDTH), BF16)],
        scratch_shapes=[pltpu.VMEM((n_rows, QK_WIDTH), BF16), pltpu.VMEM((n_rows, 1), F32),
                        pltpu.VMEM((n_rows, 1), F32), pltpu.VMEM((n_rows, V_WIDTH), F32)],
        compiler_params=pltpu.CompilerParams(dimension_semantics=("arbitrary",),
                                             vmem_limit_bytes=VMEM_LIMIT),
        name="out_mlp_sample_attn",
    )(pool, attn, x, w_out, gf, w_up, w_down, gl, *lams, subln_g, qb, kb_new, vb_new, cache_kt, cache_v)


def _keys_from_feature_major(kt, frames):
    streams = kt.shape[0]
    return jnp.transpose(kt.reshape(streams, N_HEADS, 2, HEAD_DIM, frames), (0, 4, 1, 2, 3))[None]


def kernel(x_prompt, x_sample, state_pool, cache_k, cache_v, norm_mix_g, w_in, w_pool, pool_scale,
           lambda_q1, lambda_k1, lambda_q2, lambda_k2, subln_g, w_out, norm_ffn_g, w_up, w_down,
           norm_final_g):
    assert w_in.shape[0] == 1, "one layer per call"
    B, T, _ = x_prompt.shape
    S, TS, _ = x_sample.shape
    past_len = cache_k.shape[2]
    lambda_init = 0.8 - 0.6 * math.exp(-0.3 * 0)

    g_mix = norm_mix_g[0][None]
    g_ffn = norm_ffn_g[0][None]
    g_fin = norm_final_g[None]
    g_sub = subln_g[0][None]
    pscale = pool_scale[0][None]
    lams = (lambda_q1[0][None], lambda_k1[0][None], lambda_q2[0][None], lambda_k2[0][None])
    w_in_f, w_pool_f, w_out_f = w_in[0], w_pool[0], w_out[0]
    w_up_b = w_up[0].astype(BF16)
    w_down_b = w_down[0].astype(BF16)

    zero_buf = jnp.zeros((B, TAIL, POOL_WIDTH), F32)
    kt_p, v_p, qb, kb, vtb, pool_p, tail_p = _proj_pool(
        x_prompt, zero_buf, g_mix, w_in_f, w_pool_f, pscale, pos0=0, bb=1, tm=PROJ_ROWS,
        feature_major=True)
    attn_p = _prompt_attn(qb, kb, vtb, lams, g_sub.reshape(V_DIM, 1), lambda_init=lambda_init, blk=ATTN_BLOCK)

    buf_s = jnp.pad(state_pool[0], ((0, 0), (TAIL - POOL_STATE, 0), (0, 0)))
    k_s, v_s, qb_s, kb_s, vb_s, pool_s, tail_s = _proj_pool(
        x_sample, buf_s, g_mix, w_in_f, w_pool_f, pscale, pos0=past_len, bb=S, tm=TS,
        feature_major=False)
    cache_kt = jnp.transpose(cache_k[0], (0, 2, 3, 4, 1)).reshape(S, QK_WIDTH, past_len)
    cache_vi = cache_v[0].reshape(S, past_len * N_HEADS, V_DIM)

    y_p, attn_s = _out_mlp_with_sample_attn(
        pool_p.reshape(B * T, -1), attn_p.reshape(B * T, -1), x_prompt.reshape(B * T, -1),
        w_out_f, g_ffn, w_up_b, w_down_b, g_fin, qb_s, kb_s, vb_s, cache_kt, cache_vi, lams, g_sub,
        tm=PROMPT_ROWS, lambda_init=lambda_init)
    y_p = y_p.reshape(B, T, D_MODEL)
    y_s = _out_mlp(pool_s.reshape(S * TS, -1), attn_s.reshape(S * TS, -1), x_sample.reshape(S * TS, -1),
                   w_out_f, g_ffn, w_up_b, w_down_b, g_fin, tm=S * TS).reshape(S, TS, D_MODEL)

    return (y_p, y_s,
            tail_p[:, TAIL - POOL_STATE:][None],
            _keys_from_feature_major(kt_p, T),
            v_p.reshape(1, B, T, N_HEADS, V_DIM),
            tail_s[:, TAIL - POOL_STATE:][None],
            k_s.reshape(1, S, TS, N_HEADS, 2, HEAD_DIM),
            v_s.reshape(1, S, TS, N_HEADS, V_DIM))
```

```python
import functools
import math

import jax
import jax.numpy as jnp
from jax import lax
from jax.experimental import pallas as pl
from jax.experimental.pallas import tpu as pltpu

D_MODEL = 1024
POOL_WIDTH = 512
POOL_WINDOWS = (2, 4, 8, 16)
POOL_GROUP = 128
POOL_STATE = 15
N_HEADS = 4
HEAD_DIM = 64
V_DIM = 128
QK_WIDTH = 512
V_WIDTH = N_HEADS * V_DIM
D_FF = 4096
CHUNK_BITS = 6
HEAD_DIM_BITS = 6
assert HEAD_DIM == 1 << HEAD_DIM_BITS
EPS = 1e-6
SUBLN_EPS = 1e-5

TAIL = 16
HEAD_LANES = 2 * HEAD_DIM
SUM_ROWS = 16
VT_ROWS = V_DIM + SUM_ROWS
BIAS_TERMS = 3
BIAS_ROWS = 16
assert 3 * BIAS_TERMS <= BIAS_ROWS
POS_LOW_BITS = 8

BF16 = jnp.bfloat16
F32 = jnp.float32
LOG2_E = math.log2(math.e)

PROJ_ROWS = 1024
PROMPT_ROWS = 512
ATTN_BLOCK = 256
FF_CHUNK = 1024
INTERLEAVE_GROUP = 1
SCORE_SLOTS = 3
VMEM_LIMIT = 56 * 1024 * 1024


def _nt_dot(a, b):
    return lax.dot_general(a, b, (((1,), (1,)), ((), ())), preferred_element_type=F32)


def _dot(a, b):
    return jnp.dot(a, b, preferred_element_type=F32)


def _head_slope(h):
    return lax.bitcast_convert_type((127 - 2 * (h + 1)) << 23, F32) * LOG2_E


def _lambda(lq1, lk1, lq2, lk2, lambda_init):
    return (jnp.exp(jnp.sum(lq1[...] * lk1[...], axis=-1, keepdims=True))
            - jnp.exp(jnp.sum(lq2[...] * lk2[...], axis=-1, keepdims=True)) + lambda_init)


def _sub_ln(o0, o1, lam, g, lambda_init):
    o = o0 - lam * o1
    ms = jnp.mean(o * o, axis=-1, keepdims=True)
    return (o * lax.rsqrt(ms + SUBLN_EPS) * g) * (1.0 - lambda_init)


def _proj_pool_kernel(x_ref, g_ref, w_in_ref, w_pool_ref, pscale_ref, buf_ref,
                      k_ref, v_ref, qb_ref, kb_ref, vb_ref, pool_ref, tail_ref, ext_ref,
                      *, pos0, feature_major):
    bb, tm, _ = x_ref.shape
    rows = bb * tm
    t = pl.program_id(1)

    @pl.when(t == 0)
    def _():
        ext_ref[:, 0:TAIL, :] = buf_ref[...]

    x = x_ref[...].reshape(rows, D_MODEL)
    ms = jnp.mean(x * x, axis=-1, keepdims=True)
    hn = (x * lax.rsqrt(ms + EPS) * g_ref[...]).astype(BF16)

    def proj(j):
        return _dot(hn, w_in_ref[:, j * 512:(j + 1) * 512].astype(BF16))

    ext_ref[:, TAIL:TAIL + tm, :] = proj(0).reshape(bb, tm, POOL_WIDTH)
    pos = pos0 + t * tm + lax.broadcasted_iota(jnp.int32, (1, tm, 1), 1)

    def pool_group(g):
        w = POOL_WINDOWS[g]
        lanes = slice(g * POOL_GROUP, (g + 1) * POOL_GROUP)
        ext = ext_ref[:, :, lanes]
        acc, span = ext, 1
        while span < w:
            acc = acc + pltpu.roll(acc, span, axis=1)
            span *= 2
        cnt = jnp.minimum(pos + 1, w).astype(F32)
        diff = (acc[:, TAIL:, :] / cnt - ext[:, TAIL:, :]).reshape(rows, POOL_GROUP).astype(BF16)
        y = _dot(diff, w_pool_ref[g].astype(BF16)) * pscale_ref[:, lanes]
        pool_ref[:, :, lanes] = y.astype(BF16).reshape(bb, tm, POOL_GROUP)

    k = proj(2)
    kb_ref[...] = k.astype(BF16).reshape(bb, tm, QK_WIDTH)
    if feature_major:
        assert bb == 1
        k_ref[0] = k.T
    else:
        k_ref[...] = k.reshape(bb, tm, QK_WIDTH)
    pool_group(0)
    pool_group(1)
    v = proj(3)
    if feature_major:
        vt = v.T.astype(BF16)
        for h in range(N_HEADS):
            vb_ref[0, h * VT_ROWS:h * VT_ROWS + V_DIM, :] = vt[h * V_DIM:(h + 1) * V_DIM]
            vb_ref[0, h * VT_ROWS + V_DIM:(h + 1) * VT_ROWS, :] = jnp.ones((SUM_ROWS, tm), BF16)
    else:
        vb_ref[...] = v.astype(BF16).reshape(bb, tm, V_WIDTH)
    for h in range(N_HEADS):
        v_ref[:, pl.ds(h, tm, stride=N_HEADS), :] = v[:, h * V_DIM:(h + 1) * V_DIM].reshape(bb, tm, V_DIM)
    pool_group(2)
    pool_group(3)
    q = proj(1)
    qb_ref[...] = (q * (LOG2_E / math.sqrt(HEAD_DIM))).astype(BF16).reshape(bb, tm, QK_WIDTH)

    tail = ext_ref[:, tm:tm + TAIL, :]
    tail_ref[...] = tail
    ext_ref[:, 0:TAIL, :] = tail


def _proj_pool(x, buf, g, w_in, w_pool, pscale, *, pos0, bb, tm, feature_major):
    B, T, _ = x.shape
    grid = (B // bb, T // tm)
    row_spec = lambda width: pl.BlockSpec((bb, tm, width), lambda b, t: (b, t, 0))
    const = lambda shape: pl.BlockSpec(shape, lambda b, t: (0,) * len(shape), pipeline_mode=pl.Buffered(1))
    seq_spec = pl.BlockSpec((bb, TAIL, POOL_WIDTH), lambda b, t: (b, 0, 0))
    act = lambda dtype: jax.ShapeDtypeStruct((B, T, QK_WIDTH), dtype)
    if feature_major:
        fm_spec = lambda rows: pl.BlockSpec((bb, rows, tm), lambda b, t: (b, 0, t))
        fm_shape = lambda rows, dtype: jax.ShapeDtypeStruct((B, rows, T), dtype)
        k_spec, k_shape = fm_spec(QK_WIDTH), fm_shape(QK_WIDTH, F32)
        vb_spec, vb_shape = fm_spec(N_HEADS * VT_ROWS), fm_shape(N_HEADS * VT_ROWS, BF16)
    else:
        k_spec, k_shape = row_spec(QK_WIDTH), act(F32)
        vb_spec, vb_shape = row_spec(V_WIDTH), act(BF16)
    return pl.pallas_call(
        functools.partial(_proj_pool_kernel, pos0=pos0, feature_major=feature_major),
        grid=grid,
        in_specs=[row_spec(D_MODEL), const((1, D_MODEL)), const((D_MODEL, 4 * 512)),
                  const((len(POOL_WINDOWS), POOL_GROUP, POOL_GROUP)), const((1, POOL_WIDTH)), seq_spec],
        out_specs=[k_spec, pl.BlockSpec((bb, tm * N_HEADS, V_DIM), lambda b, t: (b, t, 0)),
                   row_spec(QK_WIDTH), row_spec(QK_WIDTH), vb_spec, row_spec(POOL_WIDTH), seq_spec],
        out_shape=[k_shape, jax.ShapeDtypeStruct((B, T * N_HEADS, V_DIM), F32),
                   act(BF16), act(BF16), vb_shape, act(BF16),
                   jax.ShapeDtypeStruct((B, TAIL, POOL_WIDTH), F32)],
        scratch_shapes=[pltpu.VMEM((bb, TAIL + tm, POOL_WIDTH), F32)],
        compiler_params=pltpu.CompilerParams(dimension_semantics=("arbitrary", "arbitrary"),
                                             vmem_limit_bytes=VMEM_LIMIT),
        name="proj_pool",
    )(x, g, w_in, w_pool, pscale, buf)


def _prompt_attn_kernel(lq1, lk1, lq2, lk2, gcol_ref, q_ref, k_ref, vt_ref, o_ref,
                        key_pos_ref, bias_diag_ref, acc_ref, *s_slot, lambda_init, blk):
    n_blocks = q_ref.shape[1] // blk
    h = pl.program_id(0)
    slope = _head_slope(jnp.full((1, 1), h, jnp.int32))

    @pl.when((h == 0) & (pl.program_id(1) == 0))
    def _():
        kg = lax.broadcasted_iota(jnp.int32, key_pos_ref.shape, 0)
        lane = lax.broadcasted_iota(jnp.int32, key_pos_ref.shape, 1)
        code = jnp.where(lane < BIAS_TERMS, kg >> POS_LOW_BITS,
                         jnp.where(lane < 2 * BIAS_TERMS, kg & ((1 << POS_LOW_BITS) - 1),
                                   jnp.where(lane < 3 * BIAS_TERMS, 1, 0)))
        key_pos_ref[...] = code.astype(F32).astype(BF16)

    @pl.when(pl.program_id(1) == 0)
    def _():
        kk = lax.broadcasted_iota(jnp.int32, bias_diag_ref.shape, 0)
        qq = lax.broadcasted_iota(jnp.int32, bias_diag_ref.shape, 1) & (blk - 1)
        visible = (kk >> CHUNK_BITS) <= (qq >> CHUNK_BITS)
        bias_diag_ref[...] = jnp.where(visible, -slope * jnp.abs(qq - kk).astype(F32), -jnp.inf)

    lam = _lambda(lq1, lk1, lq2, lk2, lambda_init)
    feat = lax.broadcasted_iota(jnp.int32, (HEAD_LANES, blk), 0)
    zero = jnp.zeros((HEAD_LANES, blk), BF16)
    q_off = (lax.broadcasted_iota(jnp.int32, (1, 2 * blk), 1) & (blk - 1)).astype(F32)
    bias_row = lax.broadcasted_iota(jnp.int32, (BIAS_ROWS, 2 * blk), 0)

    def pieces(x):
        out = []
        for _ in range(BIAS_TERMS):
            piece = x.astype(BF16).astype(F32)
            out.append(piece)
            x = x - piece
        return out

    n_slots = len(s_slot)
    run_time_zero = jnp.minimum(pl.program_id(1), 0)

    def sublane_max(s):
        return jnp.max(s.reshape(blk // 8, 8, s.shape[1]), axis=0)

    def scores(c, col_max):
        n = c * blk
        qt = q_ref[0, n:n + blk, :].astype(F32).T.astype(BF16)
        q_cols = jnp.concatenate([jnp.where(feat < HEAD_DIM, qt, zero), jnp.where(feat >= HEAD_DIM, qt, zero)],
                                 axis=1)
        s_diag = _dot(k_ref[0, n:n + blk, :], q_cols) + bias_diag_ref[...]
        s_slot[c % n_slots][n:n + blk, :] = s_diag
        m = sublane_max(s_diag)
        yield
        if c > 0:
            terms = (pieces(slope * float(1 << POS_LOW_BITS)) + pieces(slope)
                     + pieces(-slope * (q_off + float(n))))
            bias_cols = jnp.zeros(bias_row.shape, F32)
            for r, term in enumerate(terms):
                bias_cols = jnp.where(bias_row == r, term, bias_cols)
            w = jnp.concatenate([q_cols, bias_cols.astype(BF16),
                                 jnp.zeros((HEAD_LANES - BIAS_ROWS, 2 * blk), BF16)], axis=0)
            for j in range(c):
                rows = slice(j * blk, (j + 1) * blk)
                s_past = _dot(jnp.concatenate([k_ref[0, rows, :], key_pos_ref[rows, :]], axis=1), w)
                s_slot[c % n_slots][rows, :] = s_past
                m = jnp.maximum(m, sublane_max(s_past))
                yield
        col_max[c] = jnp.max(m, axis=0, keepdims=True)

    def outputs(c, m):
        n = c * blk
        for j in range(c + 1):
            rows = slice(j * blk, (j + 1) * blk)
            start = pl.multiple_of(run_time_zero + j * blk, blk)
            if j < c:
                p = jnp.exp2(s_slot[c % n_slots][pl.ds(start, blk), :] - m)
            else:
                hb = blk // 2
                assert hb % (1 << CHUNK_BITS) == 0
                top = jnp.exp2(s_slot[c % n_slots][pl.ds(start, hb), :] - m)
                bottom = []
                for half in range(2):
                    cols = slice(half * blk + hb, (half + 1) * blk)
                    bottom += [jnp.zeros((hb, hb), F32),
                               jnp.exp2(s_slot[c % n_slots][pl.ds(start + hb, hb), cols] - m[:, cols])]
                p = jnp.concatenate([top, jnp.concatenate(bottom, axis=1)], axis=0)
            pv = _dot(vt_ref[0, :, rows], p.astype(BF16))
            if j == 0:
                acc_ref[...] = pv
            else:
                acc_ref[...] += pv
            yield
        acc = acc_ref[...]
        o = [acc[0:V_DIM, half * blk:(half + 1) * blk] / acc[V_DIM:V_DIM + 1, half * blk:(half + 1) * blk]
             for half in range(2)]
        ot = o[0] - lam * o[1]
        ms = jnp.mean(ot * ot, axis=0, keepdims=True)
        yt = (ot * lax.rsqrt(ms + SUBLN_EPS) * gcol_ref[...]) * (1.0 - lambda_init)
        o_ref[0, n:n + blk, :] = yt.T.astype(o_ref.dtype)

    col_max = {}
    ahead = SCORE_SLOTS - 1
    for c in range(n_blocks - 1, max(n_blocks - 1 - ahead, -1), -1):
        for _ in scores(c, col_max):
            pass
    for c in reversed(range(n_blocks)):
        gens = [outputs(c, col_max[c])]
        if c - ahead >= 0:
            gens.insert(0, scores(c - ahead, col_max))
        while gens:
            for g in list(gens):
                for _ in range(INTERLEAVE_GROUP):
                    if next(g, StopIteration) is StopIteration:
                        gens.remove(g)
                        break


def _prompt_attn(qb, kb, vtb, lams, subln_gcol, *, lambda_init, blk):
    B, T, _ = qb.shape
    grid = (N_HEADS, B)
    vec = lambda n: pl.BlockSpec((1, n), lambda h, b: (0, 0))
    qk_spec = pl.BlockSpec((1, T, HEAD_LANES), lambda h, b: (b, 0, h))
    vt_spec = pl.BlockSpec((1, VT_ROWS, T), lambda h, b: (b, h, 0))
    return pl.pallas_call(
        functools.partial(_prompt_attn_kernel, lambda_init=lambda_init, blk=blk),
        grid=grid,
        in_specs=[vec(HEAD_DIM)] * 4 + [pl.BlockSpec((V_DIM, 1), lambda h, b: (0, 0)), qk_spec, qk_spec, vt_spec],
        out_specs=qk_spec,
        out_shape=jax.ShapeDtypeStruct((B, T, V_WIDTH), BF16),
        scratch_shapes=[pltpu.VMEM((T - blk, HEAD_LANES), BF16), pltpu.VMEM((blk, 2 * blk), F32),
                        pltpu.VMEM((VT_ROWS, 2 * blk), F32)] + [pltpu.VMEM((T, 2 * blk), F32)] * SCORE_SLOTS,
        compiler_params=pltpu.CompilerParams(dimension_semantics=("arbitrary",) * 2,
                                             vmem_limit_bytes=VMEM_LIMIT),
        name="prompt_attn",
    )(*lams, subln_gcol, qb, kb, vtb)


def _sample_attn_step(j, nj, lq1, lk1, lq2, lk2, g_ref, q_ref, kn_ref, vn_ref, ckt_ref, cv_ref, o_ref,
                      qrows_ref, m_ref, l_ref, acc_ref, *, lambda_init, past_len):
    tq = q_ref.shape[1]
    head_rows = 2 * tq
    n_rows = N_HEADS * head_rows
    kb = ckt_ref.shape[2]

    tq_bits = tq.bit_length() - 1
    assert tq == 1 << tq_bits
    row = lax.broadcasted_iota(jnp.int32, (n_rows, 1), 0)
    q_rel = row & (tq - 1)
    slope = _head_slope(row >> (tq_bits + 1))

    @pl.when(j == 0)
    def _():
        q = q_ref[0]
        tiled = jnp.concatenate([q] * (2 * N_HEADS), axis=0)
        r = lax.broadcasted_iota(jnp.int32, tiled.shape, 0)
        c = lax.broadcasted_iota(jnp.int32, tiled.shape, 1)
        qrows_ref[...] = jnp.where((r >> tq_bits) == (c >> HEAD_DIM_BITS), tiled, jnp.zeros_like(tiled))
        m_ref[...] = jnp.full(m_ref.shape, -jnp.inf, F32)
        l_ref[...] = jnp.zeros(l_ref.shape, F32)
        acc_ref[...] = jnp.zeros(acc_ref.shape, F32)

    def update(s, values):
        m_old = m_ref[...]
        m_new = jnp.maximum(m_old, jnp.max(s, axis=-1, keepdims=True))
        p = jnp.exp2(s - m_new)
        alpha = jnp.exp2(m_old - m_new)
        l_ref[...] = alpha * l_ref[...] + jnp.sum(p, axis=-1, keepdims=True)
        m_ref[...] = m_new
        acc_ref[...] = alpha * acc_ref[...] + _dot(p.astype(BF16), values)

    scores = []

    def score_cache():
        k_rel = (j * kb - past_len + lax.broadcasted_iota(jnp.int32, (1, kb), 1)).astype(F32)
        scores.append(_dot(qrows_ref[...], ckt_ref[0].astype(BF16)) + slope * k_rel)

    def attend_cache():
        values = jnp.concatenate(
            [cv_ref[0, pl.ds(h, kb, stride=N_HEADS), :].astype(BF16) for h in range(N_HEADS)], axis=1)
        update(scores.pop(), values)

    def finish_stream():
        @pl.when(j == nj - 1)
        def _():
            k_new = lax.broadcasted_iota(jnp.int32, (1, tq), 1)
            update(_nt_dot(qrows_ref[...], kn_ref[0]) + slope * (q_rel - jnp.abs(q_rel - k_new)).astype(F32),
                   vn_ref[0])
            lam = _lambda(lq1, lk1, lq2, lk2, lambda_init)
            out = acc_ref[...] / l_ref[...]
            for h in range(N_HEADS):
                own = out[h * head_rows:(h + 1) * head_rows, h * V_DIM:(h + 1) * V_DIM]
                o_ref[0, :, h * V_DIM:(h + 1) * V_DIM] = _sub_ln(own[:tq], own[tq:], lam, g_ref[...],
                                                                 lambda_init).astype(o_ref.dtype)

    return score_cache, attend_cache, finish_stream


def _out_mlp_kernel(pool_ref, attn_ref, x_ref, w_out_ref, gf_ref, w_up_ref, w_down_ref, gl_ref, y_ref):
    _out_mlp_tile(pool_ref, attn_ref, x_ref, w_out_ref, gf_ref, w_up_ref, w_down_ref, gl_ref, y_ref)


def _out_mlp_tile(pool_ref, attn_ref, x_ref, w_out_ref, gf_ref, w_up_ref, w_down_ref, gl_ref, y_ref,
                  after_chunk=()):
    mixed = (_dot(pool_ref[...], w_out_ref[0:POOL_WIDTH, :].astype(BF16))
             + _dot(attn_ref[...], w_out_ref[POOL_WIDTH:, :].astype(BF16)))
    h = x_ref[...] + mixed
    y_ref[...] = h
    ms = jnp.mean(h * h, axis=-1, keepdims=True)
    hn = (h * lax.rsqrt(ms + EPS) * gf_ref[...]).astype(BF16)
    for c in range(D_FF // FF_CHUNK):
        cols = slice(c * FF_CHUNK, (c + 1) * FF_CHUNK)
        a = jnp.maximum(_dot(hn, w_up_ref[:, cols]), 0.0)
        y_ref[...] += _dot((a * a).astype(BF16), w_down_ref[cols, :])
        if c < len(after_chunk):
            after_chunk[c]()
    y = y_ref[...]
    ms = jnp.mean(y * y, axis=-1, keepdims=True)
    y_ref[...] = y * lax.rsqrt(ms + EPS) * gl_ref[...]


def _out_mlp(pool, attn, x, w_out, gf, w_up, w_down, gl, *, tm):
    rows = x.shape[0]
    row_spec = lambda width: pl.BlockSpec((tm, width), lambda r: (r, 0))
    const = lambda shape: pl.BlockSpec(shape, lambda r: (0, 0), pipeline_mode=pl.Buffered(1))
    return pl.pallas_call(
        _out_mlp_kernel,
        grid=(rows // tm,),
        in_specs=[row_spec(POOL_WIDTH), row_spec(V_WIDTH), row_spec(D_MODEL),
                  const((D_MODEL, D_MODEL)), const((1, D_MODEL)), const((D_MODEL, D_FF)),
                  const((D_FF, D_MODEL)), const((1, D_MODEL))],
        out_specs=row_spec(D_MODEL),
        out_shape=jax.ShapeDtypeStruct((rows, D_MODEL), F32),
        compiler_params=pltpu.CompilerParams(dimension_semantics=("arbitrary",),
                                             vmem_limit_bytes=VMEM_LIMIT),
        name="out_mlp",
    )(pool, attn, x, w_out, gf, w_up, w_down, gl)


def _out_mlp_sample_attn_kernel(*refs, parts, lambda_init, past_len):
    mlp_in, samp_in, (y_ref, o_ref), scratch = refs[:8], refs[8:18], refs[18:20], refs[20:]
    r = pl.program_id(0)
    score_cache, attend_cache, finish_stream = _sample_attn_step(
        r % parts, parts, *samp_in, o_ref, *scratch, lambda_init=lambda_init, past_len=past_len)
    _out_mlp_tile(*mlp_in, y_ref, after_chunk=(score_cache, attend_cache))
    finish_stream()


def _out_mlp_with_sample_attn(pool, attn, x, w_out, gf, w_up, w_down, gl,
                              qb, kb_new, vb_new, cache_kt, cache_v, lams, subln_g, *, tm, lambda_init):
    rows = x.shape[0]
    S, tq, _ = qb.shape
    past_len = cache_kt.shape[2]
    steps = rows // tm
    assert steps % S == 0, "every stream gets the same number of grid steps"
    parts = steps // S
    kblk = past_len // parts
    n_rows = 2 * N_HEADS * tq
    row_spec = lambda width: pl.BlockSpec((tm, width), lambda r: (r, 0))
    const = lambda shape: pl.BlockSpec(shape, lambda r: (0, 0), pipeline_mode=pl.Buffered(1))
    vec = lambda n: pl.BlockSpec((1, n), lambda r: (0, 0))
    new_spec = pl.BlockSpec((1, tq, QK_WIDTH), lambda r: (r // parts, 0, 0))
    kt_spec = pl.BlockSpec((1, QK_WIDTH, kblk), lambda r: (r // parts, 0, r % parts))
    v_spec = pl.BlockSpec((1, kblk * N_HEADS, V_DIM), lambda r: (r // parts, r % parts, 0))
    return pl.pallas_call(
        functools.partial(_out_mlp_sample_attn_kernel, parts=parts, lambda_init=lambda_init, past_len=past_len),
        grid=(steps,),
        in_specs=[row_spec(POOL_WIDTH), row_spec(V_WIDTH), row_spec(D_MODEL),
                  const((D_MODEL, D_MODEL)), const((1, D_MODEL)), const((D_MODEL, D_FF)),
                  const((D_FF, D_MODEL)), const((1, D_MODEL))]
                 + [vec(HEAD_DIM)] * 4 + [vec(V_DIM), new_spec, new_spec, new_spec, kt_spec, v_spec],
        out_specs=[row_spec(D_MODEL), new_spec],
        out_shape=[jax.ShapeDtypeStruct((rows, D_MODEL), F32), jax.ShapeDtypeStruct((S, tq, V_WIDTH), BF16)],
        scratch_shapes=[pltpu.VMEM((n_rows, QK_WIDTH), BF16), pltpu.VMEM((n_rows, 1), F32),
                        pltpu.VMEM((n_rows, 1), F32), pltpu.VMEM((n_rows, V_WIDTH), F32)],
        compiler_params=pltpu.CompilerParams(dimension_semantics=("arbitrary",),
                                             vmem_limit_bytes=VMEM_LIMIT),
        name="out_mlp_sample_attn",
    )(pool, attn, x, w_out, gf, w_up, w_down, gl, *lams, subln_g, qb, kb_new, vb_new, cache_kt, cache_v)


def _keys_from_feature_major(kt, frames):
    streams = kt.shape[0]
    return jnp.transpose(kt.reshape(streams, N_HEADS, 2, HEAD_DIM, frames), (0, 4, 1, 2, 3))[None]


def kernel(x_prompt, x_sample, state_pool, cache_k, cache_v, norm_mix_g, w_in, w_pool, pool_scale,
           lambda_q1, lambda_k1, lambda_q2, lambda_k2, subln_g, w_out, norm_ffn_g, w_up, w_down,
           norm_final_g):
    assert w_in.shape[0] == 1, "one layer per call"
    B, T, _ = x_prompt.shape
    S, TS, _ = x_sample.shape
    past_len = cache_k.shape[2]
    lambda_init = 0.8 - 0.6 * math.exp(-0.3 * 0)

    g_mix = norm_mix_g[0][None]
    g_ffn = norm_ffn_g[0][None]
    g_fin = norm_final_g[None]
    g_sub = subln_g[0][None]
    pscale = pool_scale[0][None]
    lams = (lambda_q1[0][None], lambda_k1[0][None], lambda_q2[0][None], lambda_k2[0][None])
    w_in_f, w_pool_f, w_out_f = w_in[0], w_pool[0], w_out[0]
    w_up_b = w_up[0].astype(BF16)
    w_down_b = w_down[0].astype(BF16)

    zero_buf = jnp.zeros((B, TAIL, POOL_WIDTH), F32)
    kt_p, v_p, qb, kb, vtb, pool_p, tail_p = _proj_pool(
        x_prompt, zero_buf, g_mix, w_in_f, w_pool_f, pscale, pos0=0, bb=1, tm=PROJ_ROWS,
        feature_major=True)
    attn_p = _prompt_attn(qb, kb, vtb, lams, g_sub.reshape(V_DIM, 1), lambda_init=lambda_init, blk=ATTN_BLOCK)

    buf_s = jnp.pad(state_pool[0], ((0, 0), (TAIL - POOL_STATE, 0), (0, 0)))
    k_s, v_s, qb_s, kb_s, vb_s, pool_s, tail_s = _proj_pool(
        x_sample, buf_s, g_mix, w_in_f, w_pool_f, pscale, pos0=past_len, bb=S, tm=TS,
        feature_major=False)
    cache_kt = jnp.transpose(cache_k[0], (0, 2, 3, 4, 1)).reshape(S, QK_WIDTH, past_len)
    cache_vi = cache_v[0].reshape(S, past_len * N_HEADS, V_DIM)

    y_p, attn_s = _out_mlp_with_sample_attn(
        pool_p.reshape(B * T, -1), attn_p.reshape(B * T, -1), x_prompt.reshape(B * T, -1),
        w_out_f, g_ffn, w_up_b, w_down_b, g_fin, qb_s, kb_s, vb_s, cache_kt, cache_vi, lams, g_sub,
        tm=PROMPT_ROWS, lambda_init=lambda_init)
    y_p = y_p.reshape(B, T, D_MODEL)
    y_s = _out_mlp(pool_s.reshape(S * TS, -1), attn_s.reshape(S * TS, -1), x_sample.reshape(S * TS, -1),
                   w_out_f, g_ffn, w_up_b, w_down_b, g_fin, tm=S * TS).reshape(S, TS, D_MODEL)

    return (y_p, y_s,
            tail_p[:, TAIL - POOL_STATE:][None],
            _keys_from_feature_major(kt_p, T),
            v_p.reshape(1, B, T, N_HEADS, V_DIM),
            tail_s[:, TAIL - POOL_STATE:][None],
            k_s.reshape(1, S, TS, N_HEADS, 2, HEAD_DIM),
            v_s.reshape(1, S, TS, N_HEADS, V_DIM))
```
